```python
import math
import jax, jax.numpy as jnp
from jax import lax
import numpy as np

D_MODEL = 1024
BATCH = 16
SEQ = 256
DEPTH = 4
DEC_BATCH = 8
DEC_SEQ = 2048
PAST_LEN = 512

GRID_W = 64
MIX_W = D_MODEL
POOL_W = D_MODEL // 4
POOL_GROUPS = 4
POOL_WINDOWS = (2, 4, 8, 16)
POOL_GW = POOL_W // POOL_GROUPS
CONV_W = D_MODEL // 4
CONV_K = 31
MLA_W = MIX_W - POOL_W - CONV_W
N_HEADS = 4
V_DIM = MLA_W // N_HEADS
NOPE_DIM = 128
ROPE_DIM = 64
Q_LORA = 384
KV_LORA = 256
IN_W = POOL_W + Q_LORA + KV_LORA + ROPE_DIM + 2 * CONV_W
D_FF = 2816
FFN_K = 3
ROPE_THETA = 10000.0
Q_BLOCK = 128
EPS = 1e-6

kernel_name = 'hybrid_pool_mla_conformer_diffusion_step'


def rmsnorm(x, g):
    xf = x.astype(jnp.float32)
    y = xf * lax.rsqrt(jnp.mean(xf * xf, axis=-1, keepdims=True) + EPS)
    return (y * g.astype(jnp.float32)).astype(x.dtype)


def modulate(h, shift, scale):
    return h * (1 + scale) + shift


def dwconv(x, w, b):
    K, C = w.shape
    y = lax.conv_general_dilated(x, w[:, None, :].astype(x.dtype), window_strides=(1,),
                                 padding=[(K // 2, K // 2)],
                                 dimension_numbers=('NWC', 'WIO', 'NWC'),
                                 feature_group_count=C)
    return y + b


def pool_mixer(z, w, scale):
    B, L, _ = z.shape
    zf = z.astype(jnp.float32).reshape(B, L, POOL_GROUPS, POOL_GW)
    cs = jnp.concatenate([jnp.zeros((B, 1, POOL_GROUPS, POOL_GW), jnp.float32),
                          jnp.cumsum(zf, axis=1)], axis=1)
    t = jnp.arange(L)
    outs = []
    for g, win in enumerate(POOL_WINDOWS):
        start = jnp.clip(t - win // 2, 0, L)
        end = jnp.clip(t - win // 2 + win, 0, L)
        s = cs[:, end, g] - cs[:, start, g]
        outs.append(s / (end - start).astype(jnp.float32)[None, :, None])
    pooled = jnp.stack(outs, axis=2)
    d = (pooled - zf).astype(z.dtype)
    y = jnp.einsum('blgc,gcd->blgd', d, w).reshape(B, L, POOL_W)
    return y * scale


def rope_angles(L):
    rows = L // GRID_W
    row = jnp.repeat(jnp.arange(rows, dtype=jnp.float32), GRID_W)
    col = jnp.tile(jnp.arange(GRID_W, dtype=jnp.float32), rows)
    n = ROPE_DIM // 4
    inv = ROPE_THETA ** (-jnp.arange(n, dtype=jnp.float32) / n)
    return row[:, None] * inv, col[:, None] * inv


def rotate(x, ang):
    n = ang.shape[-1]
    x1, x2 = x[..., :n], x[..., n:]
    cos = jnp.cos(ang).astype(x.dtype)
    sin = jnp.sin(ang).astype(x.dtype)
    return jnp.concatenate([x1 * cos - x2 * sin, x1 * sin + x2 * cos], axis=-1)


def axial_rope(x, ang_r, ang_c):
    half = ROPE_DIM // 2
    return jnp.concatenate([rotate(x[..., :half], ang_r), rotate(x[..., half:], ang_c)], axis=-1)


def mla_attend(q_nope, q_rope, k_nope, k_rope, v):
    B, Lq, H, _ = q_nope.shape
    nb = Lq // Q_BLOCK
    scale = 1.0 / math.sqrt(NOPE_DIM + ROPE_DIM)

    def to_blocks(a):
        return a.reshape(B, nb, Q_BLOCK, *a.shape[2:]).swapaxes(0, 1)

    def block(qs):
        qn, qr = qs
        s = (jnp.einsum('bqhd,bkhd->bhqk', qn, k_nope).astype(jnp.float32)
             + jnp.einsum('bqhr,bkr->bhqk', qr, k_rope).astype(jnp.float32))
        p = jax.nn.softmax(s * scale, axis=-1).astype(v.dtype)
        return jnp.einsum('bhqk,bkhd->bqhd', p, v)

    o = lax.map(block, (to_blocks(q_nope), to_blocks(q_rope)))
    return o.swapaxes(0, 1).reshape(B, Lq, H * V_DIM)


def conformer_conv(z, w, b, g, beta):
    a, gate = jnp.split(z, 2, axis=-1)
    u = dwconv(a * jax.nn.sigmoid(gate), w, b)
    uf = u.astype(jnp.float32)
    mu = jnp.mean(uf, axis=-1, keepdims=True)
    var = jnp.mean(jnp.square(uf - mu), axis=-1, keepdims=True)
    un = (uf - mu) * lax.rsqrt(var + EPS) * g.astype(jnp.float32) + beta.astype(jnp.float32)
    return jax.nn.silu(un).astype(z.dtype)


def conv_ffn(h, w_up, cw, cb, w_down):
    u = dwconv(h @ w_up, cw, cb)
    g, v = jnp.split(u, 2, axis=-1)
    return (jax.nn.silu(g) * v) @ w_down


def setup_inputs(seed: int = 0) -> dict:
    key = jax.random.key(seed)
    ks = jax.random.split(key, 32)

    def nrm(k, shape, s):
        return jax.random.normal(k, shape, jnp.float32) * s

    D = D_MODEL
    return {
        'x_prompt': nrm(ks[0], (BATCH, SEQ, D), 1.0),
        'x_sample': nrm(ks[1], (DEC_BATCH, DEC_SEQ, D), 1.0),
        'cache_ckv': nrm(ks[2], (DEC_BATCH, DEPTH, PAST_LEN, KV_LORA), 1.0),
        'cache_krope': nrm(ks[3], (DEC_BATCH, DEPTH, PAST_LEN, ROPE_DIM), 1.0),
        'c': nrm(ks[4], (DEC_BATCH, D), 1.0),
        'c_ctx': nrm(ks[5], (D,), 1.0),
        'ada_w': nrm(ks[6], (DEPTH, D, 6 * D), 0.5 * D ** -0.5),
        'ada_b': nrm(ks[7], (DEPTH, 6 * D), 0.02),
        'norm1_g': 1.0 + nrm(ks[8], (DEPTH, D), 0.05),
        'w_in': nrm(ks[9], (DEPTH, D, IN_W), D ** -0.5),
        'pool_w': nrm(ks[10], (DEPTH, POOL_GROUPS, POOL_GW, POOL_GW), POOL_GW ** -0.5),
        'pool_scale': 1.0 + nrm(ks[11], (DEPTH, POOL_W), 0.1),
        'q_norm_g': 1.0 + nrm(ks[12], (DEPTH, Q_LORA), 0.05),
        'w_uq': nrm(ks[13], (DEPTH, Q_LORA, N_HEADS * (NOPE_DIM + ROPE_DIM)), Q_LORA ** -0.5),
        'kv_norm_g': 1.0 + nrm(ks[14], (DEPTH, KV_LORA), 0.05),
        'w_ukv': nrm(ks[15], (DEPTH, KV_LORA, N_HEADS * (NOPE_DIM + V_DIM)), KV_LORA ** -0.5),
        'conv_w': nrm(ks[16], (DEPTH, CONV_K, CONV_W), CONV_K ** -0.5),
        'conv_b': nrm(ks[17], (DEPTH, CONV_W), 0.02),
        'conv_ln_g': 1.0 + nrm(ks[18], (DEPTH, CONV_W), 0.05),
        'conv_ln_b': nrm(ks[19], (DEPTH, CONV_W), 0.02),
        'w_out': nrm(ks[20], (DEPTH, MIX_W, D), MIX_W ** -0.5),
        'norm2_g': 1.0 + nrm(ks[21], (DEPTH, D), 0.05),
        'w_up': nrm(ks[22], (DEPTH, D, 2 * D_FF), D ** -0.5),
        'ffn_conv_w': nrm(ks[23], (DEPTH, FFN_K, 2 * D_FF), FFN_K ** -0.5),
        'ffn_conv_b': nrm(ks[24], (DEPTH, 2 * D_FF), 0.02),
        'w_down': nrm(ks[25], (DEPTH, D_FF, D), D_FF ** -0.5),
        'final_g': 1.0 + nrm(ks[26], (D,), 0.05),
    }


def reference(x_prompt, x_sample, cache_ckv, cache_krope, c, c_ctx, ada_w, ada_b, norm1_g, w_in,
              pool_w, pool_scale, q_norm_g, w_uq, kv_norm_g, w_ukv, conv_w, conv_b, conv_ln_g,
              conv_ln_b, w_out, norm2_g, w_up, ffn_conv_w, ffn_conv_b, w_down, final_g):

    def layer(x, mod, l, ang, ctx):
        B, L, _ = x.shape
        sh1, sc1, g1, sh2, sc2, g2 = jnp.split(mod, 6, axis=-1)
        h = modulate(rmsnorm(x, norm1_g[l]), sh1, sc1)
        z = h @ w_in[l]
        o1 = POOL_W
        o2 = o1 + Q_LORA
        o3 = o2 + KV_LORA
        o4 = o3 + ROPE_DIM
        zp, zq, zkv, zkr, zc = z[..., :o1], z[..., o1:o2], z[..., o2:o3], z[..., o3:o4], z[..., o4:]
        y_pool = pool_mixer(zp, pool_w[l], pool_scale[l])
        q = (rmsnorm(zq, q_norm_g[l]) @ w_uq[l]).reshape(B, L, N_HEADS, NOPE_DIM + ROPE_DIM)
        q_nope, q_rope = q[..., :NOPE_DIM], q[..., NOPE_DIM:]
        ckv = rmsnorm(zkv, kv_norm_g[l])
        kv = (ckv @ w_ukv[l]).reshape(B, L, N_HEADS, NOPE_DIM + V_DIM)
        k_nope, v = kv[..., :NOPE_DIM], kv[..., NOPE_DIM:]
        k_rope = zkr
        if ang is not None:
            ang_r, ang_c = ang
            q_rope = axial_rope(q_rope, ang_r[:, None, :], ang_c[:, None, :])
            k_rope = axial_rope(k_rope, ang_r, ang_c)
        if ctx is not None:
            ctx_ckv, ctx_kr = ctx
            Lc = ctx_ckv.shape[1]
            kv_c = (ctx_ckv @ w_ukv[l]).reshape(B, Lc, N_HEADS, NOPE_DIM + V_DIM)
            k_nope = jnp.concatenate([k_nope, kv_c[..., :NOPE_DIM]], axis=1)
            v = jnp.concatenate([v, kv_c[..., NOPE_DIM:]], axis=1)
            k_rope = jnp.concatenate([k_rope, ctx_kr], axis=1)
        y_att = mla_attend(q_nope, q_rope, k_nope, k_rope, v)
        y_conv = conformer_conv(zc, conv_w[l], conv_b[l], conv_ln_g[l], conv_ln_b[l])
        y_mix = jnp.concatenate([y_pool, y_att, y_conv], axis=-1) @ w_out[l]
        x = x + g1 * y_mix
        h2 = modulate(rmsnorm(x, norm2_g[l]), sh2, sc2)
        x = x + g2 * conv_ffn(h2, w_up[l], ffn_conv_w[l], ffn_conv_b[l], w_down[l])
        return x, ckv, zkr

    xc = x_prompt
    ckv_list = []
    kr_list = []
    for l in range(DEPTH):
        mod = (jax.nn.silu(c_ctx) @ ada_w[l] + ada_b[l])[None, None, :]
        xc, ckv, kr = layer(xc, mod, l, None, None)
        ckv_list.append(ckv)
        kr_list.append(kr)
    y_prompt = rmsnorm(xc, final_g)
    state_ckv = jnp.stack(ckv_list, axis=1)
    state_krope = jnp.stack(kr_list, axis=1)

    ang = rope_angles(x_sample.shape[1])
    xs = x_sample
    for l in range(DEPTH):
        mod = (jax.nn.silu(c) @ ada_w[l] + ada_b[l])[:, None, :]
        xs, _, _ = layer(xs, mod, l, ang, (cache_ckv[:, l], cache_krope[:, l]))
    y_sample = rmsnorm(xs, final_g)

    return (y_prompt, y_sample, state_ckv, state_krope)
```

```python
import functools
import math

import numpy as np
import jax
import jax.numpy as jnp
from jax import lax
from jax.experimental import pallas as pl
from jax.experimental.pallas import tpu as pltpu

D_MODEL = 1024
DEPTH = 4
GRID_W = 64
POOL_W = 256
POOL_GROUPS = 4
POOL_GW = 64
POOL_WINDOWS = (2, 4, 8, 16)
CONV_W = 256
CONV_K = 31
MLA_W = 512
N_HEADS = 4
V_DIM = 128
NOPE_DIM = 128
ROPE_DIM = 64
Q_LORA = 384
KV_LORA = 256
IN_W = POOL_W + Q_LORA + KV_LORA + ROPE_DIM + 2 * CONV_W
D_FF = 2816
FFN_K = 3
ROPE_THETA = 10000.0
EPS = 1e-6

LANES = 128
HEAD_W = 2 * LANES
IN_W_PAD = IN_W + ROPE_DIM
HALO = 16
FF_CHUNK = 256
N_FF_CHUNKS = D_FF // FF_CHUNK
VMEM_LIMIT = 56 * 1024 * 1024

F32 = jnp.float32
BF16 = jnp.bfloat16


def _bf(x):
    return x.astype(BF16)


def _dot(a, b):
    return jnp.dot(a, b, preferred_element_type=F32)


def _rms(x, g):
    return x * lax.rsqrt(jnp.mean(x * x, axis=-1, keepdims=True) + EPS) * g


def _silu(x):
    return x * jax.nn.sigmoid(x)


def _params(n_parallel):
    return pltpu.CompilerParams(dimension_semantics=("parallel",) * n_parallel,
                                vmem_limit_bytes=VMEM_LIMIT)


def _mod_kernel(c_ref, w_ref, b_ref, o_ref):
    o_ref[0] = _dot(_bf(_silu(c_ref[...])), _bf(w_ref[0])) + b_ref[0]


def _mod_call(c_all, ada_w, ada_b):
    R = c_all.shape[0]
    tn = 1536
    return pl.pallas_call(
        _mod_kernel,
        grid=(DEPTH, 6 * D_MODEL // tn),
        in_specs=[pl.BlockSpec((R, D_MODEL), lambda l, j: (0, 0)),
                  pl.BlockSpec((1, D_MODEL, tn), lambda l, j: (l, 0, j)),
                  pl.BlockSpec((1, 1, tn), lambda l, j: (l, 0, j))],
        out_specs=pl.BlockSpec((1, R, tn), lambda l, j: (l, 0, j)),
        out_shape=jax.ShapeDtypeStruct((DEPTH, R, 6 * D_MODEL), F32),
        compiler_params=_params(2),
        name="adaln_mod",
    )(c_all, ada_w, ada_b)


def _inproj_kernel(rope, state, x_ref, mod_ref, g1_ref, win_ref, qg_ref, wuq_ref, kvg_ref, wukv_ref,
                   *rest):
    if rope:
        rope_ref, rest = rest[0], rest[1:]
    zp_ref, zc_ref, q_ref, k_ref, v_ref = rest[:5]
    x = x_ref[0]
    mod = mod_ref[0]
    h = _rms(x, g1_ref[...]) * (1.0 + mod[1:2]) + mod[0:1]
    z = _dot(_bf(h), win_ref[...])
    o1 = POOL_W
    o2 = o1 + Q_LORA
    o3 = o2 + KV_LORA
    o4 = o3 + 2 * CONV_W
    zp_ref[0] = z[:, :o1]
    zc_ref[0] = z[:, o3:o4]
    kr = z[:, o4:]
    scale = 1.0 / math.sqrt(NOPE_DIM + ROPE_DIM)
    qn = _rms(z[:, o1:o2], qg_ref[...] * scale)
    q = _dot(_bf(qn), wuq_ref[...])
    ckv = _rms(z[:, o2:o3], kvg_ref[...])
    kv = _dot(_bf(ckv), wukv_ref[...])
    lane = lax.broadcasted_iota(jnp.int32, kr.shape, 1)
    if rope:
        tab = rope_ref[...]
        qa, qb, ka = tab[:, :LANES], tab[:, LANES:2 * LANES], tab[:, 2 * LANES:]
        krot = kr * ka + pltpu.roll(kr, ROPE_DIM, 1) * qb
    else:
        krot = jnp.where(lane < ROPE_DIM, kr, 0.0)
    for hd in range(N_HEADS):
        c0 = hd * HEAD_W
        q0 = q[:, c0:c0 + LANES]
        q1 = q[:, c0 + LANES:c0 + HEAD_W]
        if rope:
            q0 = q0 * qa + q1 * qb
        q_ref[0, :, c0:c0 + LANES] = _bf(q0)
        q_ref[0, :, c0 + LANES:c0 + HEAD_W] = _bf(q1)
        k_ref[0, :, c0:c0 + LANES] = _bf(kv[:, c0:c0 + LANES] + krot)
        k_ref[0, :, c0 + LANES:c0 + HEAD_W] = _bf(kv[:, c0 + LANES:c0 + HEAD_W])
    v_ref[0] = _bf(kv[:, N_HEADS * HEAD_W:])
    if state:
        ckv_ref, kr_ref = rest[5:7]
        ckv_ref[0] = ckv
        kr_ref[0] = kr[:, :ROPE_DIM]


def _inproj_call(x, mod, per_batch_mod, lw, rope_tab, state, tm):
    B, L, D = x.shape
    rope = rope_tab is not None
    full = lambda a: pl.BlockSpec(a.shape, lambda b, i: (0,) * a.ndim)
    tile = lambda w: pl.BlockSpec((1, tm, w), lambda b, i: (b, i, 0))
    mod_map = (lambda b, i: (b, 0, 0)) if per_batch_mod else (lambda b, i: (0, 0, 0))
    weights = [lw["g1"], lw["w_in"], lw["qg"], lw["w_uq"], lw["kvg"], lw["w_ukv"]]
    in_specs = [tile(D), pl.BlockSpec((1, 8, D), mod_map)] + [full(w) for w in weights]
    args = [x, mod] + weights
    if rope:
        in_specs.append(pl.BlockSpec((tm, 3 * LANES), lambda b, i: (i, 0)))
        args.append(rope_tab)
    widths = [(POOL_W, F32), (2 * CONV_W, F32), (N_HEADS * HEAD_W, BF16), (N_HEADS * HEAD_W, BF16),
              (N_HEADS * V_DIM, BF16)]
    if state:
        widths += [(KV_LORA, F32), (ROPE_DIM, F32)]
    return pl.pallas_call(
        functools.partial(_inproj_kernel, rope, state),
        grid=(B, L // tm),
        in_specs=in_specs,
        out_specs=[tile(w) for w, _ in widths],
        out_shape=[jax.ShapeDtypeStruct((B, L, w), dt) for w, dt in widths],
        compiler_params=_params(2),
        name="inproj",
    )(*args)


def _cachekv_kernel(ckv_ref, kr_ref, wukv_ref, place_ref, k_ref, v_ref):
    kv = _dot(_bf(ckv_ref[0, 0]), wukv_ref[...])
    krot = _dot(_bf(kr_ref[0, 0]), place_ref[...])
    for hd in range(N_HEADS):
        c0 = hd * HEAD_W
        k_ref[0, :, c0:c0 + LANES] = _bf(kv[:, c0:c0 + LANES] + krot)
        k_ref[0, :, c0 + LANES:c0 + HEAD_W] = _bf(kv[:, c0 + LANES:c0 + HEAD_W])
    v_ref[0] = _bf(kv[:, N_HEADS * HEAD_W:])


def _cachekv_call(cache_ckv, cache_krope, layer, w_ukv, place):
    B, _, Lc, _ = cache_ckv.shape
    return pl.pallas_call(
        _cachekv_kernel,
        grid=(B,),
        in_specs=[pl.BlockSpec((1, 1, Lc, KV_LORA), lambda b: (b, layer, 0, 0)),
                  pl.BlockSpec((1, 1, Lc, ROPE_DIM), lambda b: (b, layer, 0, 0)),
                  pl.BlockSpec(w_ukv.shape, lambda b: (0, 0)),
                  pl.BlockSpec(place.shape, lambda b: (0, 0))],
        out_specs=[pl.BlockSpec((1, Lc, N_HEADS * HEAD_W), lambda b: (b, 0, 0)),
                   pl.BlockSpec((1, Lc, N_HEADS * V_DIM), lambda b: (b, 0, 0))],
        out_shape=[jax.ShapeDtypeStruct((B, Lc, N_HEADS * HEAD_W), BF16),
                   jax.ShapeDtypeStruct((B, Lc, N_HEADS * V_DIM), BF16)],
        compiler_params=_params(1),
        name="cache_kv",
    )(cache_ckv, cache_krope, w_ukv, place)


def _nt_dot(a, b):
    return lax.dot_general(a, b, (((1,), (1,)), ((), ())), preferred_element_type=F32)


def _attn_kernel(ctx, q_ref, k_ref, v_ref, *rest):
    q = q_ref[0]
    s = _nt_dot(q, k_ref[0])
    m = jnp.max(s, axis=-1, keepdims=True)
    if ctx:
        kc_ref, vc_ref, o_ref = rest
        sc = _nt_dot(q, kc_ref[0])
        m = jnp.maximum(m, jnp.max(sc, axis=-1, keepdims=True))
    else:
        (o_ref,) = rest
    p = jnp.exp(s - m)
    l = jnp.sum(p, axis=-1, keepdims=True)
    o = _dot(_bf(p), v_ref[0])
    if ctx:
        pc = jnp.exp(sc - m)
        l = l + jnp.sum(pc, axis=-1, keepdims=True)
        o = o + _dot(_bf(pc), vc_ref[0])
    o_ref[0] = _bf(o / l)


def _attn_call(q, k, v, kc, vc, tq):
    B, L, _ = q.shape
    ctx = kc is not None
    kv_spec = lambda a, w: pl.BlockSpec((1, a.shape[1], w), lambda b, h, i: (b, 0, h))
    in_specs = [pl.BlockSpec((1, tq, HEAD_W), lambda b, h, i: (b, i, h)),
                kv_spec(k, HEAD_W), kv_spec(v, V_DIM)]
    args = [q, k, v]
    if ctx:
        in_specs += [kv_spec(kc, HEAD_W), kv_spec(vc, V_DIM)]
        args += [kc, vc]
    return pl.pallas_call(
        functools.partial(_attn_kernel, ctx),
        grid=(B, N_HEADS, L // tq),
        in_specs=in_specs,
        out_specs=pl.BlockSpec((1, tq, V_DIM), lambda b, h, i: (b, i, h)),
        out_shape=jax.ShapeDtypeStruct((B, L, N_HEADS * V_DIM), BF16),
        compiler_params=_params(3),
        name="attention",
    )(*args)


def _fill_halo(dst_ref, src_ref, fn, i, n_tiles, tm):
    t0 = pl.multiple_of(i * tm, tm)
    w = dst_ref.shape[1]
    dst_ref[HALO:HALO + tm, :] = fn(src_ref[0, pl.ds(t0, tm), :])

    @pl.when(i > 0)
    def _():
        dst_ref[0:HALO, :] = fn(src_ref[0, pl.ds(pl.multiple_of(jnp.maximum(t0 - HALO, 0), HALO), HALO), :])

    @pl.when(i == 0)
    def _():
        dst_ref[0:HALO, :] = jnp.zeros((HALO, w), dst_ref.dtype)

    @pl.when(i < n_tiles - 1)
    def _():
        nxt = jnp.minimum(t0 + tm, (n_tiles - 1) * tm)
        dst_ref[HALO + tm:, :] = fn(src_ref[0, pl.ds(pl.multiple_of(nxt, HALO), HALO), :])

    @pl.when(i == n_tiles - 1)
    def _():
        dst_ref[HALO + tm:, :] = jnp.zeros((HALO, w), dst_ref.dtype)


def _mix_kernel(tm, seq_len, x_ref, zp_ref, zc_ref, att_ref, mod_ref, pw_ref, ps_ref, cw_ref, cb_ref,
                lng_ref, lnb_ref, wout_ref, g2_ref, xo_ref, h2_ref, pbuf, cbuf):
    i = pl.program_id(1)
    n_tiles = seq_len // tm
    _fill_halo(pbuf, zp_ref, lambda a: a, i, n_tiles, tm)
    _fill_halo(cbuf, zc_ref, lambda a: a[:, :CONV_W] * jax.nn.sigmoid(a[:, CONV_W:]), i, n_tiles, tm)

    z = pbuf[...]
    n = tm + 2 * HALO
    t = i * tm + lax.broadcasted_iota(jnp.int32, (tm, LANES), 0)
    lane = lax.broadcasted_iota(jnp.int32, (tm, LANES), 1)

    def count(win):
        return (jnp.minimum(t + win // 2, seq_len) - jnp.maximum(t - win // 2, 0)).astype(F32)

    zl, zr = z[:, :LANES], z[:, LANES:]
    a2l = zl[0:n - 1] + zl[1:n]
    a4l = a2l[0:n - 3] + a2l[2:n - 1]
    a2r = zr[0:n - 1] + zr[1:n]
    a4r = a2r[0:n - 3] + a2r[2:n - 1]
    a8r = a4r[0:n - 7] + a4r[4:n - 3]
    a16r = a8r[0:n - 15] + a8r[8:n - 7]
    low = lane < POOL_GW
    pooled_l = jnp.where(low, a2l[HALO - 1:HALO - 1 + tm] / count(2), a4l[HALO - 2:HALO - 2 + tm] / count(4))
    pooled_r = jnp.where(low, a8r[HALO - 4:HALO - 4 + tm] / count(8), a16r[HALO - 8:HALO - 8 + tm] / count(16))
    zc_l, zc_r = zl[HALO:HALO + tm], zr[HALO:HALO + tm]
    d = jnp.concatenate([pooled_l - zc_l, pooled_r - zc_r], axis=1)
    y_pool = _dot(_bf(d), pw_ref[...]) * ps_ref[...]

    rows = 32
    convs = []
    for r0 in range(0, tm, rows):
        acc = jnp.zeros((rows, CONV_W), F32) + cb_ref[...]
        for kk in range(CONV_K):
            start = r0 + HALO - CONV_K // 2 + kk
            acc = acc + cbuf[start:start + rows, :] * cw_ref[kk:kk + 1, :]
        convs.append(acc)
    u = jnp.concatenate(convs, axis=0)
    mu = jnp.mean(u, axis=-1, keepdims=True)
    var = jnp.mean(jnp.square(u - mu), axis=-1, keepdims=True)
    y_conv = _silu((u - mu) * lax.rsqrt(var + EPS) * lng_ref[...] + lnb_ref[...])

    o1, o2 = POOL_W, POOL_W + MLA_W
    y_mix = (_dot(_bf(y_pool), wout_ref[0:o1, :]) + _dot(att_ref[0], wout_ref[o1:o2, :])
             + _dot(_bf(y_conv), wout_ref[o2:, :]))
    mod = mod_ref[0]
    x = x_ref[0] + mod[2:3] * y_mix
    xo_ref[0] = x
    h2_ref[0] = _bf(_rms(x, g2_ref[...]) * (1.0 + mod[4:5]) + mod[3:4])


def _mix_call(x, zp, zc, att, mod, per_batch_mod, lw, tm):
    B, L, D = x.shape
    full = lambda a: pl.BlockSpec(a.shape, lambda b, i: (0,) * a.ndim)
    tile = lambda w: pl.BlockSpec((1, tm, w), lambda b, i: (b, i, 0))
    seq = lambda w: pl.BlockSpec((1, L, w), lambda b, i: (b, 0, 0))
    mod_map = (lambda b, i: (b, 0, 0)) if per_batch_mod else (lambda b, i: (0, 0, 0))
    weights = [lw["pool_bd"], lw["pool_scale"], lw["conv_w"], lw["conv_b"], lw["ln_g"], lw["ln_b"],
               lw["w_out"], lw["g2"]]
    return pl.pallas_call(
        functools.partial(_mix_kernel, tm, L),
        grid=(B, L // tm),
        in_specs=[tile(D), seq(POOL_W), seq(2 * CONV_W), tile(MLA_W), pl.BlockSpec((1, 8, D), mod_map)]
        + [full(w) for w in weights],
        out_specs=[tile(D), tile(D)],
        out_shape=[jax.ShapeDtypeStruct((B, L, D), F32), jax.ShapeDtypeStruct((B, L, D), BF16)],
        scratch_shapes=[pltpu.VMEM((tm + 2 * HALO, POOL_W), F32), pltpu.VMEM((tm + 2 * HALO, CONV_W), F32)],
        compiler_params=_params(2),
        name="mixer",
    )(x, zp, zc, att, mod, *weights)


def _ffn_kernel(tm, seq_len, final, x_ref, h2_ref, mod_ref, wup_ref, cw_ref, cb_ref, wdn_ref, fg_ref,
                o_ref, hbuf, acc_ref):
    i = pl.program_id(1)
    _fill_halo(hbuf, h2_ref, lambda a: a, i, seq_len // tm, tm)
    acc_ref[...] = jnp.zeros_like(acc_ref)

    def chunk(j, carry):
        u = _dot(hbuf[...], wup_ref[j])
        cw = cw_ref[j]
        c = (u[HALO - 1:HALO - 1 + tm] * cw[0:1] + u[HALO:HALO + tm] * cw[1:2]
             + u[HALO + 1:HALO + 1 + tm] * cw[2:3] + cb_ref[j])
        act = _silu(c[:, :FF_CHUNK]) * c[:, FF_CHUNK:]
        acc_ref[...] += _dot(_bf(act), wdn_ref[j])
        return carry

    lax.fori_loop(0, N_FF_CHUNKS, chunk, 0)
    x = x_ref[0] + mod_ref[0][5:6] * acc_ref[...]
    o_ref[0] = _rms(x, fg_ref[...]) if final else x


def _ffn_call(x, h2, mod, per_batch_mod, lw, final_g, final, tm):
    B, L, D = x.shape
    once = lambda a: pl.BlockSpec(a.shape, lambda b, i: (0,) * a.ndim, pipeline_mode=pl.Buffered(1))
    tile = lambda w: pl.BlockSpec((1, tm, w), lambda b, i: (b, i, 0))
    mod_map = (lambda b, i: (b, 0, 0)) if per_batch_mod else (lambda b, i: (0, 0, 0))
    weights = [lw["w_up"], lw["ffn_cw"], lw["ffn_cb"], lw["w_down"], final_g]
    return pl.pallas_call(
        functools.partial(_ffn_kernel, tm, L, final),
        grid=(B, L // tm),
        in_specs=[tile(D), pl.BlockSpec((1, L, D), lambda b, i: (b, 0, 0)), pl.BlockSpec((1, 8, D), mod_map)]
        + [once(w) for w in weights],
        out_specs=tile(D),
        out_shape=jax.ShapeDtypeStruct((B, L, D), F32),
        scratch_shapes=[pltpu.VMEM((tm + 2 * HALO, D), BF16), pltpu.VMEM((tm, D), F32)],
        compiler_params=_params(2),
        name="conv_ffn",
    )(x, h2, mod, *weights)


def _layout_indices():
    sw = np.arange(ROPE_DIM) ^ (ROPE_DIM // 4)
    o3 = POOL_W + Q_LORA + KV_LORA
    o4 = o3 + ROPE_DIM
    idx_in = np.concatenate([np.arange(0, o3), np.arange(o4, IN_W), np.arange(o3, o4), o3 + sw])
    qh = NOPE_DIM + ROPE_DIM
    half = NOPE_DIM // 2
    idx_uq = np.concatenate([
        np.concatenate([h * qh + NOPE_DIM + np.arange(ROPE_DIM), h * qh + np.arange(half),
                        h * qh + NOPE_DIM + sw, h * qh + half + np.arange(half)])
        for h in range(N_HEADS)])
    kvh = NOPE_DIM + V_DIM
    zero_col = N_HEADS * kvh
    zeros = np.full(half, zero_col)
    idx_k = np.concatenate([
        np.concatenate([zeros, h * kvh + np.arange(half), zeros, h * kvh + half + np.arange(half)])
        for h in range(N_HEADS)])
    idx_v = np.concatenate([h * kvh + NOPE_DIM + np.arange(V_DIM) for h in range(N_HEADS)])
    return idx_in, idx_uq, np.concatenate([idx_k, idx_v])


def _prep_layer(l, p):
    idx_in, idx_uq, idx_ukv = _layout_indices()
    row = lambda a: a[l][None, :]
    w_ukv_aug = jnp.concatenate([p["w_ukv"][l], jnp.zeros((KV_LORA, 1), F32)], axis=1)
    eye = jnp.eye(POOL_GROUPS, dtype=F32)
    pool_bd = (eye[:, None, :, None] * p["pool_w"][l][:, :, None, :]).reshape(POOL_W, POOL_W)
    w_up = p["w_up"][l].reshape(D_MODEL, 2, N_FF_CHUNKS, FF_CHUNK).transpose(2, 0, 1, 3)
    cw = p["ffn_conv_w"][l].reshape(FFN_K, 2, N_FF_CHUNKS, FF_CHUNK).transpose(2, 0, 1, 3)
    cw = jnp.pad(cw.reshape(N_FF_CHUNKS, FFN_K, 2 * FF_CHUNK), ((0, 0), (0, 8 - FFN_K), (0, 0)))
    cb = p["ffn_conv_b"][l].reshape(2, N_FF_CHUNKS, FF_CHUNK).transpose(1, 0, 2)
    return {
        "g1": row(p["norm1_g"]),
        "w_in": _bf(p["w_in"][l][:, idx_in]),
        "qg": row(p["q_norm_g"]),
        "w_uq": _bf(p["w_uq"][l][:, idx_uq]),
        "kvg": row(p["kv_norm_g"]),
        "w_ukv": _bf(w_ukv_aug[:, idx_ukv]),
        "pool_bd": _bf(pool_bd),
        "pool_scale": row(p["pool_scale"]),
        "conv_w": jnp.pad(p["conv_w"][l], ((0, 1), (0, 0))),
        "conv_b": row(p["conv_b"]),
        "ln_g": row(p["conv_ln_g"]),
        "ln_b": row(p["conv_ln_b"]),
        "w_out": _bf(p["w_out"][l]),
        "g2": row(p["norm2_g"]),
        "w_up": _bf(w_up.reshape(N_FF_CHUNKS, D_MODEL, 2 * FF_CHUNK)),
        "ffn_cw": cw,
        "ffn_cb": cb.reshape(N_FF_CHUNKS, 1, 2 * FF_CHUNK),
        "w_down": _bf(p["w_down"][l].reshape(N_FF_CHUNKS, FF_CHUNK, D_MODEL)),
    }


def _rope_table(L):
    rows = L // GRID_W
    row = jnp.repeat(jnp.arange(rows, dtype=F32), GRID_W)
    col = jnp.tile(jnp.arange(GRID_W, dtype=F32), rows)
    n = ROPE_DIM // 4
    inv = ROPE_THETA ** (-jnp.arange(n, dtype=F32) / n)
    ang_r, ang_c = row[:, None] * inv, col[:, None] * inv
    cos = jnp.concatenate([jnp.cos(ang_r)] * 2 + [jnp.cos(ang_c)] * 2, axis=1)
    sin = jnp.concatenate([-jnp.sin(ang_r), jnp.sin(ang_r), -jnp.sin(ang_c), jnp.sin(ang_c)], axis=1)
    one, zero = jnp.ones_like(cos), jnp.zeros_like(cos)
    return jnp.concatenate([cos, one, sin, zero, cos, zero], axis=1)


def kernel(x_prompt, x_sample, cache_ckv, cache_krope, c, c_ctx, ada_w, ada_b, norm1_g, w_in, pool_w,
           pool_scale, q_norm_g, w_uq, kv_norm_g, w_ukv, conv_w, conv_b, conv_ln_g, conv_ln_b, w_out,
           norm2_g, w_up, ffn_conv_w, ffn_conv_b, w_down, final_g):
    p = dict(norm1_g=norm1_g, w_in=w_in, pool_w=pool_w, pool_scale=pool_scale, q_norm_g=q_norm_g,
             w_uq=w_uq, kv_norm_g=kv_norm_g, w_ukv=w_ukv, conv_w=conv_w, conv_b=conv_b,
             conv_ln_g=conv_ln_g, conv_ln_b=conv_ln_b, w_out=w_out, norm2_g=norm2_g, w_up=w_up,
             ffn_conv_w=ffn_conv_w, ffn_conv_b=ffn_conv_b, w_down=w_down)
    n_dec = c.shape[0]
    c_all = jnp.concatenate([c, c_ctx[None, :], jnp.zeros((16 - n_dec - 1, D_MODEL), F32)], axis=0)
    mods = _mod_call(c_all, ada_w, ada_b.reshape(DEPTH, 1, 6 * D_MODEL))
    mods = jnp.pad(mods.reshape(DEPTH, 16, 6, D_MODEL), ((0, 0), (0, 0), (0, 2), (0, 0)))
    layers = [_prep_layer(l, p) for l in range(DEPTH)]
    fg = final_g[None, :]
    place = _bf(jnp.eye(ROPE_DIM, LANES, dtype=F32))
    rope_tab = _rope_table(x_sample.shape[1])

    def run(x, mod_of, per_batch_mod, rope, cache, tm, tq, tf):
        states = []
        for l in range(DEPTH):
            lw, mod = layers[l], mod_of(l)
            outs = _inproj_call(x, mod, per_batch_mod, lw, rope, cache is None, tm)
            zp, zc, q, k, v = outs[:5]
            if cache is None:
                states.append(outs[5:])
                att = _attn_call(q, k, v, None, None, tq)
            else:
                kc, vc = _cachekv_call(cache[0], cache[1], l, lw["w_ukv"], place)
                att = _attn_call(q, k, v, kc, vc, tq)
            x, h2 = _mix_call(x, zp, zc, att, mod, per_batch_mod, lw, tm)
            x = _ffn_call(x, h2, mod, per_batch_mod, lw, fg, l == DEPTH - 1, tf)
        return x, states

    y_prompt, states = run(x_prompt, lambda l: mods[l, n_dec:n_dec + 1], False, None, None, 256, 256, 256)
    y_sample, _ = run(x_sample, lambda l: mods[l, :n_dec], True, rope_tab, (cache_ckv, cache_krope),
                      256, 256, 512)
    state_ckv = jnp.stack([s[0] for s in states], axis=1)
    state_krope = jnp.stack([s[1] for s in states], axis=1)
    return (y_prompt, y_sample, state_ckv, state_krope)
```

```python
import functools
import math

import numpy as np
import jax
import jax.numpy as jnp
from jax import lax
from jax.experimental import pallas as pl
from jax.experimental.pallas import tpu as pltpu

D_MODEL = 1024
DEPTH = 4
GRID_W = 64
POOL_W = 256
POOL_GROUPS = 4
POOL_GW = 64
POOL_WINDOWS = (2, 4, 8, 16)
CONV_W = 256
CONV_K = 31
MLA_W = 512
N_HEADS = 4
V_DIM = 128
NOPE_DIM = 128
ROPE_DIM = 64
Q_LORA = 384
KV_LORA = 256
IN_W = POOL_W + Q_LORA + KV_LORA + ROPE_DIM + 2 * CONV_W
D_FF = 2816
FFN_K = 3
ROPE_THETA = 10000.0
EPS = 1e-6

LANES = 128
HEAD_W = 2 * LANES
IN_W_PAD = IN_W + ROPE_DIM
HALO = 16
FF_CHUNK = 256
N_FF_CHUNKS = D_FF // FF_CHUNK
VMEM_LIMIT = 56 * 1024 * 1024

F32 = jnp.float32
BF16 = jnp.bfloat16


def _bf(x):
    return x.astype(BF16)


def _dot(a, b):
    return jnp.dot(a, b, preferred_element_type=F32)


def _rms(x, g):
    return x * lax.rsqrt(jnp.mean(x * x, axis=-1, keepdims=True) + EPS) * g


def _silu(x):
    return x * jax.nn.sigmoid(x)


def _params(n_parallel):
    return pltpu.CompilerParams(dimension_semantics=("parallel",) * n_parallel,
                                vmem_limit_bytes=VMEM_LIMIT)


def _mod_kernel(c_ref, w_ref, b_ref, o_ref):
    o_ref[0] = _dot(_bf(_silu(c_ref[...])), _bf(w_ref[0])) + b_ref[0]


def _mod_call(c_all, ada_w, ada_b):
    R = c_all.shape[0]
    tn = 1536
    return pl.pallas_call(
        _mod_kernel,
        grid=(DEPTH, 6 * D_MODEL // tn),
        in_specs=[pl.BlockSpec((R, D_MODEL), lambda l, j: (0, 0)),
                  pl.BlockSpec((1, D_MODEL, tn), lambda l, j: (l, 0, j)),
                  pl.BlockSpec((1, 1, tn), lambda l, j: (l, 0, j))],
        out_specs=pl.BlockSpec((1, R, tn), lambda l, j: (l, 0, j)),
        out_shape=jax.ShapeDtypeStruct((DEPTH, R, 6 * D_MODEL), F32),
        compiler_params=_params(2),
        name="adaln_mod",
    )(c_all, ada_w, ada_b)


def _inproj_kernel(rope, state, x_ref, mod_ref, g1_ref, win_ref, qg_ref, wuq_ref, kvg_ref, wukv_ref,
                   *rest):
    if rope:
        rope_ref, rest = rest[0], rest[1:]
    zp_ref, zc_ref, q_ref, k_ref, v_ref = rest[:5]
    x = x_ref[0]
    mod = mod_ref[0]
    h = _rms(x, g1_ref[...]) * (1.0 + mod[1:2]) + mod[0:1]
    z = _dot(_bf(h), win_ref[...])
    o1 = POOL_W
    o2 = o1 + Q_LORA
    o3 = o2 + KV_LORA
    o4 = o3 + 2 * CONV_W
    zp_ref[0] = z[:, :o1]
    zc_ref[0] = z[:, o3:o4]
    kr = z[:, o4:]
    scale = math.log2(math.e) / math.sqrt(NOPE_DIM + ROPE_DIM)
    qn = _rms(z[:, o1:o2], qg_ref[...] * scale)
    q = _dot(_bf(qn), wuq_ref[...])
    ckv = _rms(z[:, o2:o3], kvg_ref[...])
    kv = _dot(_bf(ckv), wukv_ref[...])
    lane = lax.broadcasted_iota(jnp.int32, kr.shape, 1)
    if rope:
        tab = rope_ref[...]
        qa, qb, ka = tab[:, :LANES], tab[:, LANES:2 * LANES], tab[:, 2 * LANES:]
        krot = kr * ka + pltpu.roll(kr, ROPE_DIM, 1) * qb
    else:
        krot = jnp.where(lane < ROPE_DIM, kr, 0.0)
    for hd in range(N_HEADS):
        c0 = hd * HEAD_W
        q0 = q[:, c0:c0 + LANES]
        q1 = q[:, c0 + LANES:c0 + HEAD_W]
        if rope:
            q0 = q0 * qa + q1 * qb
        q_ref[0, :, c0:c0 + LANES] = _bf(q0)
        q_ref[0, :, c0 + LANES:c0 + HEAD_W] = _bf(q1)
        k_ref[0, :, c0:c0 + LANES] = _bf(kv[:, c0:c0 + LANES] + krot)
        k_ref[0, :, c0 + LANES:c0 + HEAD_W] = _bf(kv[:, c0 + LANES:c0 + HEAD_W])
    v_ref[0] = _bf(kv[:, N_HEADS * HEAD_W:].T)
    if state:
        ckv_ref, kr_ref = rest[5:7]
        ckv_ref[0] = ckv
        kr_ref[0] = kr[:, :ROPE_DIM]


def _inproj_call(x, mod, per_batch_mod, lw, rope_tab, state, tm):
    B, L, D = x.shape
    rope = rope_tab is not None
    full = lambda a: pl.BlockSpec(a.shape, lambda b, i: (0,) * a.ndim)
    tile = lambda w: pl.BlockSpec((1, tm, w), lambda b, i: (b, i, 0))
    mod_map = (lambda b, i: (b, 0, 0)) if per_batch_mod else (lambda b, i: (0, 0, 0))
    weights = [lw["g1"], lw["w_in"], lw["qg"], lw["w_uq"], lw["kvg"], lw["w_ukv"]]
    in_specs = [tile(D), pl.BlockSpec((1, 8, D), mod_map)] + [full(w) for w in weights]
    args = [x, mod] + weights
    if rope:
        in_specs.append(pl.BlockSpec((tm, 3 * LANES), lambda b, i: (i, 0)))
        args.append(rope_tab)
    widths = [(POOL_W, F32), (2 * CONV_W, F32), (N_HEADS * HEAD_W, BF16), (N_HEADS * HEAD_W, BF16)]
    states = [(KV_LORA, F32), (ROPE_DIM, F32)] if state else []
    vt_spec = pl.BlockSpec((1, N_HEADS * V_DIM, tm), lambda b, i: (b, 0, i))
    vt_shape = jax.ShapeDtypeStruct((B, N_HEADS * V_DIM, L), BF16)
    return pl.pallas_call(
        functools.partial(_inproj_kernel, rope, state),
        grid=(B, L // tm),
        in_specs=in_specs,
        out_specs=[tile(w) for w, _ in widths] + [vt_spec] + [tile(w) for w, _ in states],
        out_shape=[jax.ShapeDtypeStruct((B, L, w), dt) for w, dt in widths] + [vt_shape]
        + [jax.ShapeDtypeStruct((B, L, w), dt) for w, dt in states],
        compiler_params=_params(2),
        name="inproj",
    )(*args)


def _cachekv_kernel(ckv_ref, kr_ref, wukv_ref, place_ref, k_ref, v_ref):
    kv = _dot(_bf(ckv_ref[0, 0]), wukv_ref[...])
    krot = _dot(_bf(kr_ref[0, 0]), place_ref[...])
    for hd in range(N_HEADS):
        c0 = hd * HEAD_W
        k_ref[0, :, c0:c0 + LANES] = _bf(kv[:, c0:c0 + LANES] + krot)
        k_ref[0, :, c0 + LANES:c0 + HEAD_W] = _bf(kv[:, c0 + LANES:c0 + HEAD_W])
    v_ref[0] = _bf(kv[:, N_HEADS * HEAD_W:].T)


def _cachekv_call(cache_ckv, cache_krope, layer, w_ukv, place):
    B, _, Lc, _ = cache_ckv.shape
    return pl.pallas_call(
        _cachekv_kernel,
        grid=(B,),
        in_specs=[pl.BlockSpec((1, 1, Lc, KV_LORA), lambda b: (b, layer, 0, 0)),
                  pl.BlockSpec((1, 1, Lc, ROPE_DIM), lambda b: (b, layer, 0, 0)),
                  pl.BlockSpec(w_ukv.shape, lambda b: (0, 0)),
                  pl.BlockSpec(place.shape, lambda b: (0, 0))],
        out_specs=[pl.BlockSpec((1, Lc, N_HEADS * HEAD_W), lambda b: (b, 0, 0)),
                   pl.BlockSpec((1, N_HEADS * V_DIM, Lc), lambda b: (b, 0, 0))],
        out_shape=[jax.ShapeDtypeStruct((B, Lc, N_HEADS * HEAD_W), BF16),
                   jax.ShapeDtypeStruct((B, N_HEADS * V_DIM, Lc), BF16)],
        compiler_params=_params(1),
        name="cache_kv",
    )(cache_ckv, cache_krope, w_ukv, place)


def _nt_dot(a, b):
    return lax.dot_general(a, b, (((1,), (1,)), ((), ())), preferred_element_type=F32)


def _attn_kernel(ctx, q_ref, k_ref, v_ref, *rest):
    q = q_ref[0]
    s = _nt_dot(k_ref[0], q)
    m = jnp.max(s, axis=0, keepdims=True)
    if ctx:
        kc_ref, vc_ref, o_ref = rest
        sc = _nt_dot(kc_ref[0], q)
        m = jnp.maximum(m, jnp.max(sc, axis=0, keepdims=True))
    else:
        (o_ref,) = rest
    p = jnp.exp2(s - m)
    l = jnp.sum(p, axis=0, keepdims=True)
    o = _dot(v_ref[0], _bf(p))
    if ctx:
        pc = jnp.exp2(sc - m)
        l = l + jnp.sum(pc, axis=0, keepdims=True)
        o = o + _dot(vc_ref[0], _bf(pc))
    o_ref[0] = _bf((o / l).T)


def _attn_call(q, k, v, kc, vc, tq):
    B, L, _ = q.shape
    ctx = kc is not None
    k_spec = lambda a: pl.BlockSpec((1, a.shape[1], HEAD_W), lambda b, h, i: (b, 0, h))
    v_spec = lambda a: pl.BlockSpec((1, V_DIM, a.shape[2]), lambda b, h, i: (b, h, 0))
    in_specs = [pl.BlockSpec((1, tq, HEAD_W), lambda b, h, i: (b, i, h)), k_spec(k), v_spec(v)]
    args = [q, k, v]
    if ctx:
        in_specs += [k_spec(kc), v_spec(vc)]
        args += [kc, vc]
    return pl.pallas_call(
        functools.partial(_attn_kernel, ctx),
        grid=(B, N_HEADS, L // tq),
        in_specs=in_specs,
        out_specs=pl.BlockSpec((1, tq, V_DIM), lambda b, h, i: (b, i, h)),
        out_shape=jax.ShapeDtypeStruct((B, L, N_HEADS * V_DIM), BF16),
        compiler_params=_params(3),
        name="attention",
    )(*args)


def _fill_halo(dst_ref, src_ref, fn, i, n_tiles, tm):
    t0 = pl.multiple_of(i * tm, tm)
    w = dst_ref.shape[1]
    dst_ref[HALO:HALO + tm, :] = fn(src_ref[0, pl.ds(t0, tm), :])

    @pl.when(i > 0)
    def _():
        dst_ref[0:HALO, :] = fn(src_ref[0, pl.ds(pl.multiple_of(jnp.maximum(t0 - HALO, 0), HALO), HALO), :])

    @pl.when(i == 0)
    def _():
        dst_ref[0:HALO, :] = jnp.zeros((HALO, w), dst_ref.dtype)

    @pl.when(i < n_tiles - 1)
    def _():
        nxt = jnp.minimum(t0 + tm, (n_tiles - 1) * tm)
        dst_ref[HALO + tm:, :] = fn(src_ref[0, pl.ds(pl.multiple_of(nxt, HALO), HALO), :])

    @pl.when(i == n_tiles - 1)
    def _():
        dst_ref[HALO + tm:, :] = jnp.zeros((HALO, w), dst_ref.dtype)


def _mix_kernel(tm, seq_len, x_ref, zp_ref, zc_ref, att_ref, mod_ref, pw_ref, ps_ref, cw_ref, cb_ref,
                lng_ref, lnb_ref, wout_ref, g2_ref, xo_ref, h2_ref, pbuf, cbuf):
    i = pl.program_id(1)
    n_tiles = seq_len // tm
    _fill_halo(pbuf, zp_ref, lambda a: a, i, n_tiles, tm)
    _fill_halo(cbuf, zc_ref, lambda a: a[:, :CONV_W] * jax.nn.sigmoid(a[:, CONV_W:]), i, n_tiles, tm)

    z = pbuf[...]
    n = tm + 2 * HALO
    t = i * tm + lax.broadcasted_iota(jnp.int32, (tm, LANES), 0)
    lane = lax.broadcasted_iota(jnp.int32, (tm, LANES), 1)

    def count(win):
        return (jnp.minimum(t + win // 2, seq_len) - jnp.maximum(t - win // 2, 0)).astype(F32)

    zl, zr = z[:, :LANES], z[:, LANES:]
    a2l = zl[0:n - 1] + zl[1:n]
    a4l = a2l[0:n - 3] + a2l[2:n - 1]
    a2r = zr[0:n - 1] + zr[1:n]
    a4r = a2r[0:n - 3] + a2r[2:n - 1]
    a8r = a4r[0:n - 7] + a4r[4:n - 3]
    a16r = a8r[0:n - 15] + a8r[8:n - 7]
    low = lane < POOL_GW
    pooled_l = jnp.where(low, a2l[HALO - 1:HALO - 1 + tm] / count(2), a4l[HALO - 2:HALO - 2 + tm] / count(4))
    pooled_r = jnp.where(low, a8r[HALO - 4:HALO - 4 + tm] / count(8), a16r[HALO - 8:HALO - 8 + tm] / count(16))
    zc_l, zc_r = zl[HALO:HALO + tm], zr[HALO:HALO + tm]
    d = jnp.concatenate([pooled_l - zc_l, pooled_r - zc_r], axis=1)
    y_pool = _dot(_bf(d), pw_ref[...]) * ps_ref[...]

    rows = 32
    convs = []
    for r0 in range(0, tm, rows):
        acc = jnp.zeros((rows, CONV_W), F32) + cb_ref[...]
        for kk in range(CONV_K):
            start = r0 + HALO - CONV_K // 2 + kk
            acc = acc + cbuf[start:start + rows, :] * cw_ref[kk:kk + 1, :]
        convs.append(acc)
    u = jnp.concatenate(convs, axis=0)
    mu = jnp.mean(u, axis=-1, keepdims=True)
    var = jnp.mean(jnp.square(u - mu), axis=-1, keepdims=True)
    y_conv = _silu((u - mu) * lax.rsqrt(var + EPS) * lng_ref[...] + lnb_ref[...])

    o1, o2 = POOL_W, POOL_W + MLA_W
    y_mix = (_dot(_bf(y_pool), wout_ref[0:o1, :]) + _dot(att_ref[0], wout_ref[o1:o2, :])
             + _dot(_bf(y_conv), wout_ref[o2:, :]))
    mod = mod_ref[0]
    x = x_ref[0] + mod[2:3] * y_mix
    xo_ref[0] = x
    h2_ref[0] = _bf(_rms(x, g2_ref[...]) * (1.0 + mod[4:5]) + mod[3:4])


def _mix_call(x, zp, zc, att, mod, per_batch_mod, lw, tm):
    B, L, D = x.shape
    full = lambda a: pl.BlockSpec(a.shape, lambda b, i: (0,) * a.ndim)
    tile = lambda w: pl.BlockSpec((1, tm, w), lambda b, i: (b, i, 0))
    seq = lambda w: pl.BlockSpec((1, L, w), lambda b, i: (b, 0, 0))
    mod_map = (lambda b, i: (b, 0, 0)) if per_batch_mod else (lambda b, i: (0, 0, 0))
    weights = [lw["pool_bd"], lw["pool_scale"], lw["conv_w"], lw["conv_b"], lw["ln_g"], lw["ln_b"],
               lw["w_out"], lw["g2"]]
    return pl.pallas_call(
        functools.partial(_mix_kernel, tm, L),
        grid=(B, L // tm),
        in_specs=[tile(D), seq(POOL_W), seq(2 * CONV_W), tile(MLA_W), pl.BlockSpec((1, 8, D), mod_map)]
        + [full(w) for w in weights],
        out_specs=[tile(D), tile(D)],
        out_shape=[jax.ShapeDtypeStruct((B, L, D), F32), jax.ShapeDtypeStruct((B, L, D), BF16)],
        scratch_shapes=[pltpu.VMEM((tm + 2 * HALO, POOL_W), F32), pltpu.VMEM((tm + 2 * HALO, CONV_W), F32)],
        compiler_params=_params(2),
        name="mixer",
    )(x, zp, zc, att, mod, *weights)


SUBLANES = 8


def _ffn_kernel(nq, rows, final, x_ref, h2_ref, mod_ref, wup_ref, cw_ref, cb_ref, wdn_ref, fg_ref,
                o_ref, hbuf, stage, pbuf, abuf, ybuf):
    tm = nq * rows
    seg = tm // SUBLANES
    per_piece = SUBLANES // nq
    pitch = seg + 2 * HALO + 4
    n_slabs = D_MODEL // LANES

    if nq == 1:
        _fill_halo(hbuf, h2_ref, lambda a: a, pl.program_id(1), pl.num_programs(1), tm)
    else:
        for q in range(nq + 1):
            hbuf[q * (rows + HALO):q * (rows + HALO) + HALO, :] = jnp.zeros((HALO, D_MODEL), BF16)
        for q in range(nq):
            hbuf[q * (rows + HALO) + HALO:(q + 1) * (rows + HALO), :] = h2_ref[q]

    for s in range(SUBLANES):
        src = (s // per_piece) * (rows + HALO) + (s % per_piece) * seg
        for k in range(n_slabs):
            stage[k, s * pitch:s * pitch + seg + 2 * HALO, :] = (
                hbuf[src:src + seg + 2 * HALO, k * LANES:(k + 1) * LANES].astype(F32))

    def permute(g, carry):
        for k in range(n_slabs):
            lo = stage[k, pl.ds(HALO - 1 + 2 * g, SUBLANES, stride=pitch), :]
            hi = stage[k, pl.ds(HALO + 2 * g, SUBLANES, stride=pitch), :]
            pbuf[pl.ds(pl.multiple_of(g * 2 * SUBLANES, 2 * SUBLANES), 2 * SUBLANES), k * LANES:(k + 1) * LANES] = (
                _bf(jnp.concatenate([lo, hi], axis=0)))
        return carry

    lax.fori_loop(0, seg // 2 + 1, permute, 0)

    for j in range(N_FF_CHUNKS):
        u = _dot(pbuf[...], wup_ref[j])
        cw = cw_ref[j]
        c = (u[0:tm] * cw[0:1] + u[SUBLANES:tm + SUBLANES] * cw[1:2]
             + u[2 * SUBLANES:tm + 2 * SUBLANES] * cw[2:3] + cb_ref[j])
        abuf[:, j * FF_CHUNK:(j + 1) * FF_CHUNK] = _bf(_silu(c[:, :FF_CHUNK]) * c[:, FF_CHUNK:])

    y = _dot(abuf[...], wdn_ref[...])
    for k in range(n_slabs):
        ybuf[k] = y[:, k * LANES:(k + 1) * LANES]
    gate = mod_ref[0][5:6]
    for s in range(SUBLANES):
        q, r0 = s // per_piece, (s % per_piece) * seg
        ys = jnp.concatenate(
            [jnp.concatenate([ybuf[k, pl.ds(s + SUBLANES * SUBLANES * m, SUBLANES, stride=SUBLANES), :]
                              for m in range(seg // SUBLANES)], axis=0) for k in range(n_slabs)], axis=1)
        x = x_ref[q, r0:r0 + seg, :] + gate * ys
        o_ref[q, r0:r0 + seg, :] = _rms(x, fg_ref[...]) if final else x


def _ffn_call(x, h2, mod, per_batch_mod, lw, final_g, final, nq, rows):
    B, L, D = x.shape
    assert (nq == 1 and L % rows == 0) or (rows == L and B % nq == 0)
    tm = nq * rows
    seg = tm // SUBLANES
    once = lambda a: pl.BlockSpec(a.shape, lambda b, i: (0,) * a.ndim, pipeline_mode=pl.Buffered(1))
    tile = pl.BlockSpec((nq, rows, D), lambda b, i: (b, i, 0))
    h2_spec = pl.BlockSpec((nq, L, D), lambda b, i: (b, 0, 0))
    mod_map = (lambda b, i: (b, 0, 0)) if per_batch_mod else (lambda b, i: (0, 0, 0))
    weights = [lw["w_up"], lw["ffn_cw"], lw["ffn_cb"], lw["w_down"], final_g]
    return pl.pallas_call(
        functools.partial(_ffn_kernel, nq, rows, final),
        grid=(B // nq, L // rows),
        in_specs=[tile, h2_spec, pl.BlockSpec((1, 8, D), mod_map)] + [once(w) for w in weights],
        out_specs=tile,
        out_shape=jax.ShapeDtypeStruct((B, L, D), F32),
        scratch_shapes=[pltpu.VMEM((nq * (rows + HALO) + HALO, D), BF16),
                        pltpu.VMEM((D // LANES, SUBLANES * (seg + 2 * HALO + 4), LANES), F32),
                        pltpu.VMEM(((seg + 2) * SUBLANES, D), BF16),
                        pltpu.VMEM((tm, D_FF), BF16),
                        pltpu.VMEM((D // LANES, tm, LANES), F32)],
        compiler_params=_params(2),
        name="conv_ffn",
    )(x, h2, mod, *weights)


def _layout_indices():
    sw = np.arange(ROPE_DIM) ^ (ROPE_DIM // 4)
    o3 = POOL_W + Q_LORA + KV_LORA
    o4 = o3 + ROPE_DIM
    idx_in = np.concatenate([np.arange(0, o3), np.arange(o4, IN_W), np.arange(o3, o4), o3 + sw])
    qh = NOPE_DIM + ROPE_DIM
    half = NOPE_DIM // 2
    idx_uq = np.concatenate([
        np.concatenate([h * qh + NOPE_DIM + np.arange(ROPE_DIM), h * qh + np.arange(half),
                        h * qh + NOPE_DIM + sw, h * qh + half + np.arange(half)])
        for h in range(N_HEADS)])
    kvh = NOPE_DIM + V_DIM
    zero_col = N_HEADS * kvh
    zeros = np.full(half, zero_col)
    idx_k = np.concatenate([
        np.concatenate([zeros, h * kvh + np.arange(half), zeros, h * kvh + half + np.arange(half)])
        for h in range(N_HEADS)])
    idx_v = np.concatenate([h * kvh + NOPE_DIM + np.arange(V_DIM) for h in range(N_HEADS)])
    return idx_in, idx_uq, np.concatenate([idx_k, idx_v])


def _prep_layer(l, p):
    idx_in, idx_uq, idx_ukv = _layout_indices()
    row = lambda a: a[l][None, :]
    w_ukv_aug = jnp.concatenate([p["w_ukv"][l], jnp.zeros((KV_LORA, 1), F32)], axis=1)
    eye = jnp.eye(POOL_GROUPS, dtype=F32)
    pool_bd = (eye[:, None, :, None] * p["pool_w"][l][:, :, None, :]).reshape(POOL_W, POOL_W)
    w_up = p["w_up"][l].reshape(D_MODEL, 2, N_FF_CHUNKS, FF_CHUNK).transpose(2, 0, 1, 3)
    cw = p["ffn_conv_w"][l].reshape(FFN_K, 2, N_FF_CHUNKS, FF_CHUNK).transpose(2, 0, 1, 3)
    cw = jnp.pad(cw.reshape(N_FF_CHUNKS, FFN_K, 2 * FF_CHUNK), ((0, 0), (0, 8 - FFN_K), (0, 0)))
    cb = p["ffn_conv_b"][l].reshape(2, N_FF_CHUNKS, FF_CHUNK).transpose(1, 0, 2)
    return {
        "g1": row(p["norm1_g"]),
        "w_in": _bf(p["w_in"][l][:, idx_in]),
        "qg": row(p["q_norm_g"]),
        "w_uq": _bf(p["w_uq"][l][:, idx_uq]),
        "kvg": row(p["kv_norm_g"]),
        "w_ukv": _bf(w_ukv_aug[:, idx_ukv]),
        "pool_bd": _bf(pool_bd),
        "pool_scale": row(p["pool_scale"]),
        "conv_w": jnp.pad(p["conv_w"][l], ((0, 1), (0, 0))),
        "conv_b": row(p["conv_b"]),
        "ln_g": row(p["conv_ln_g"]),
        "ln_b": row(p["conv_ln_b"]),
        "w_out": _bf(p["w_out"][l]),
        "g2": row(p["norm2_g"]),
        "w_up": _bf(w_up.reshape(N_FF_CHUNKS, D_MODEL, 2 * FF_CHUNK)),
        "ffn_cw": cw,
        "ffn_cb": cb.reshape(N_FF_CHUNKS, 1, 2 * FF_CHUNK),
        "w_down": _bf(p["w_down"][l]),
    }


def _rope_table(L):
    rows = L // GRID_W
    row = jnp.repeat(jnp.arange(rows, dtype=F32), GRID_W)
    col = jnp.tile(jnp.arange(GRID_W, dtype=F32), rows)
    n = ROPE_DIM // 4
    inv = ROPE_THETA ** (-jnp.arange(n, dtype=F32) / n)
    ang_r, ang_c = row[:, None] * inv, col[:, None] * inv
    cos = jnp.concatenate([jnp.cos(ang_r)] * 2 + [jnp.cos(ang_c)] * 2, axis=1)
    sin = jnp.concatenate([-jnp.sin(ang_r), jnp.sin(ang_r), -jnp.sin(ang_c), jnp.sin(ang_c)], axis=1)
    one, zero = jnp.ones_like(cos), jnp.zeros_like(cos)
    return jnp.concatenate([cos, one, sin, zero, cos, zero], axis=1)


def kernel(x_prompt, x_sample, cache_ckv, cache_krope, c, c_ctx, ada_w, ada_b, norm1_g, w_in, pool_w,
           pool_scale, q_norm_g, w_uq, kv_norm_g, w_ukv, conv_w, conv_b, conv_ln_g, conv_ln_b, w_out,
           norm2_g, w_up, ffn_conv_w, ffn_conv_b, w_down, final_g):
    p = dict(norm1_g=norm1_g, w_in=w_in, pool_w=pool_w, pool_scale=pool_scale, q_norm_g=q_norm_g,
             w_uq=w_uq, kv_norm_g=kv_norm_g, w_ukv=w_ukv, conv_w=conv_w, conv_b=conv_b,
             conv_ln_g=conv_ln_g, conv_ln_b=conv_ln_b, w_out=w_out, norm2_g=norm2_g, w_up=w_up,
             ffn_conv_w=ffn_conv_w, ffn_conv_b=ffn_conv_b, w_down=w_down)
    n_dec = c.shape[0]
    c_all = jnp.concatenate([c, c_ctx[None, :], jnp.zeros((16 - n_dec - 1, D_MODEL), F32)], axis=0)
    mods = _mod_call(c_all, ada_w, ada_b.reshape(DEPTH, 1, 6 * D_MODEL))
    mods = jnp.pad(mods.reshape(DEPTH, 16, 6, D_MODEL), ((0, 0), (0, 0), (0, 2), (0, 0)))
    layers = [_prep_layer(l, p) for l in range(DEPTH)]
    fg = final_g[None, :]
    place = _bf(jnp.eye(ROPE_DIM, LANES, dtype=F32))
    rope_tab = _rope_table(x_sample.shape[1])

    def run(x, mod_of, per_batch_mod, rope, cache, tm, tq, tf):
        states = []
        for l in range(DEPTH):
            lw, mod = layers[l], mod_of(l)
            outs = _inproj_call(x, mod, per_batch_mod, lw, rope, cache is None, tm)
            zp, zc, q, k, v = outs[:5]
            if cache is None:
                states.append(outs[5:])
                att = _attn_call(q, k, v, None, None, tq)
            else:
                kc, vc = _cachekv_call(cache[0], cache[1], l, lw["w_ukv"], place)
                att = _attn_call(q, k, v, kc, vc, tq)
            x, h2 = _mix_call(x, zp, zc, att, mod, per_batch_mod, lw, tm)
            x = _ffn_call(x, h2, mod, per_batch_mod, lw, fg, l == DEPTH - 1, *tf)
        return x, states

    y_prompt, states = run(x_prompt, lambda l: mods[l, n_dec:n_dec + 1], False, None, None, 256, 256, (2, 256))
    y_sample, _ = run(x_sample, lambda l: mods[l, :n_dec], True, rope_tab, (cache_ckv, cache_krope),
                      256, 1024, (1, 512))
    state_ckv = jnp.stack([s[0] for s in states], axis=1)
    state_krope = jnp.stack([s[1] for s in states], axis=1)
    return (y_prompt, y_sample, state_ckv, state_krope)
```

```python
import functools
import math

import numpy as np
import jax
import jax.numpy as jnp
from jax import lax
from jax.experimental import pallas as pl
from jax.experimental.pallas import tpu as pltpu

D_MODEL = 1024
DEPTH = 4
GRID_W = 64
POOL_W = 256
POOL_GROUPS = 4
POOL_GW = 64
POOL_WINDOWS = (2, 4, 8, 16)
CONV_W = 256
CONV_K = 31
MLA_W = 512
N_HEADS = 4
V_DIM = 128
NOPE_DIM = 128
ROPE_DIM = 64
Q_LORA = 384
KV_LORA = 256
IN_W = POOL_W + Q_LORA + KV_LORA + ROPE_DIM + 2 * CONV_W
D_FF = 2816
FFN_K = 3
ROPE_THETA = 10000.0
EPS = 1e-6

LANES = 128
SUBLANES = 8
HEAD_W = 2 * LANES
IN_W_PAD = IN_W + ROPE_DIM
HALO = 16
FF_CHUNK = 256
N_FF_CHUNKS = D_FF // FF_CHUNK
VMEM_LIMIT = 56 * 1024 * 1024

CTX_TILES = {"inproj": 256, "attn": (256, 256), "mix": 256, "ffn": (2, 256)}
DEC_TILES = {"inproj": 512, "attn": (2048, 512), "mix": 512, "ffn": (1, 512)}

F32 = jnp.float32
BF16 = jnp.bfloat16


def _bf(x):
    return x.astype(BF16)


def _dot(a, b):
    return jnp.dot(a, b, preferred_element_type=F32)


def _rms(x, g):
    return x * lax.rsqrt(jnp.mean(x * x, axis=-1, keepdims=True) + EPS) * g


def _silu(x):
    return x * jax.nn.sigmoid(x)


def _params(n_parallel):
    return pltpu.CompilerParams(dimension_semantics=("parallel",) * n_parallel,
                                vmem_limit_bytes=VMEM_LIMIT)


def _mod_kernel(c_ref, w_ref, b_ref, o_ref):
    o_ref[0] = _dot(_bf(_silu(c_ref[...])), _bf(w_ref[0])) + b_ref[0]


def _mod_call(c_all, ada_w, ada_b):
    R = c_all.shape[0]
    tn = 1536
    return pl.pallas_call(
        _mod_kernel,
        grid=(DEPTH, 6 * D_MODEL // tn),
        in_specs=[pl.BlockSpec((R, D_MODEL), lambda l, j: (0, 0)),
                  pl.BlockSpec((1, D_MODEL, tn), lambda l, j: (l, 0, j)),
                  pl.BlockSpec((1, 1, tn), lambda l, j: (l, 0, j))],
        out_specs=pl.BlockSpec((1, R, tn), lambda l, j: (l, 0, j)),
        out_shape=jax.ShapeDtypeStruct((DEPTH, R, 6 * D_MODEL), F32),
        compiler_params=_params(2),
        name="adaln_mod",
    )(c_all, ada_w, ada_b)


def _inproj_kernel(rope, state, x_ref, mod_ref, g1_ref, win_ref, qg_ref, wuq_ref, kvg_ref, wukv_ref,
                   *rest):
    if rope:
        rope_ref, rest = rest[0], rest[1:]
    zp_ref, zc_ref, q_ref, k_ref, v_ref = rest[:5]
    x = x_ref[0]
    mod = mod_ref[0]
    h = _rms(x, g1_ref[...]) * (1.0 + mod[1:2]) + mod[0:1]
    z = _dot(_bf(h), win_ref[...])
    o1 = POOL_W
    o2 = o1 + Q_LORA
    o3 = o2 + KV_LORA
    o4 = o3 + 2 * CONV_W
    zp_ref[0] = z[:, :o1]
    zc_ref[0] = z[:, o3:o4]
    kr = z[:, o4:]
    scale = math.log2(math.e) / math.sqrt(NOPE_DIM + ROPE_DIM)
    qn = _rms(z[:, o1:o2], qg_ref[...] * scale)
    q = _dot(_bf(qn), wuq_ref[...])
    ckv = _rms(z[:, o2:o3], kvg_ref[...])
    kv = _dot(_bf(ckv), wukv_ref[...])
    lane = lax.broadcasted_iota(jnp.int32, kr.shape, 1)
    if rope:
        tab = rope_ref[...]
        qa, qb, ka = tab[:, :LANES], tab[:, LANES:2 * LANES], tab[:, 2 * LANES:]
        krot = kr * ka + pltpu.roll(kr, ROPE_DIM, 1) * qb
    else:
        krot = jnp.where(lane < ROPE_DIM, kr, 0.0)
    for hd in range(N_HEADS):
        c0 = hd * HEAD_W
        q0 = q[:, c0:c0 + LANES]
        q1 = q[:, c0 + LANES:c0 + HEAD_W]
        if rope:
            q0 = q0 * qa + q1 * qb
        q_ref[0, :, c0:c0 + LANES] = _bf(q0)
        q_ref[0, :, c0 + LANES:c0 + HEAD_W] = _bf(q1)
        k_ref[0, :, c0:c0 + LANES] = _bf(kv[:, c0:c0 + LANES] + krot)
        k_ref[0, :, c0 + LANES:c0 + HEAD_W] = _bf(kv[:, c0 + LANES:c0 + HEAD_W])
    v_ref[0] = _bf(kv[:, N_HEADS * HEAD_W:].T)
    if state:
        ckv_ref, kr_ref = rest[5:7]
        ckv_ref[0] = ckv
        kr_ref[0] = kr[:, :ROPE_DIM]


def _inproj_call(x, mod, per_batch_mod, lw, rope_tab, state, tm):
    B, L, D = x.shape
    rope = rope_tab is not None
    full = lambda a: pl.BlockSpec(a.shape, lambda b, i: (0,) * a.ndim)
    tile = lambda w: pl.BlockSpec((1, tm, w), lambda b, i: (b, i, 0))
    mod_map = (lambda b, i: (b, 0, 0)) if per_batch_mod else (lambda b, i: (0, 0, 0))
    weights = [lw["g1"], lw["w_in"], lw["qg"], lw["w_uq"], lw["kvg"], lw["w_ukv"]]
    in_specs = [tile(D), pl.BlockSpec((1, 8, D), mod_map)] + [full(w) for w in weights]
    args = [x, mod] + weights
    if rope:
        in_specs.append(pl.BlockSpec((tm, 3 * LANES), lambda b, i: (i, 0)))
        args.append(rope_tab)
    widths = [(POOL_W, F32), (2 * CONV_W, F32), (N_HEADS * HEAD_W, BF16), (N_HEADS * HEAD_W, BF16)]
    states = [(KV_LORA, F32), (ROPE_DIM, F32)] if state else []
    vt_spec = pl.BlockSpec((1, N_HEADS * V_DIM, tm), lambda b, i: (b, 0, i))
    vt_shape = jax.ShapeDtypeStruct((B, N_HEADS * V_DIM, L), BF16)
    return pl.pallas_call(
        functools.partial(_inproj_kernel, rope, state),
        grid=(B, L // tm),
        in_specs=in_specs,
        out_specs=[tile(w) for w, _ in widths] + [vt_spec] + [tile(w) for w, _ in states],
        out_shape=[jax.ShapeDtypeStruct((B, L, w), dt) for w, dt in widths] + [vt_shape]
        + [jax.ShapeDtypeStruct((B, L, w), dt) for w, dt in states],
        compiler_params=_params(2),
        name="inproj",
    )(*args)


def _cachekv_kernel(ckv_ref, kr_ref, wukv_ref, place_ref, k_ref, v_ref):
    kv = _dot(_bf(ckv_ref[0, 0]), wukv_ref[...])
    krot = _dot(_bf(kr_ref[0, 0]), place_ref[...])
    for hd in range(N_HEADS):
        c0 = hd * HEAD_W
        k_ref[0, :, c0:c0 + LANES] = _bf(kv[:, c0:c0 + LANES] + krot)
        k_ref[0, :, c0 + LANES:c0 + HEAD_W] = _bf(kv[:, c0 + LANES:c0 + HEAD_W])
    v_ref[0] = _bf(kv[:, N_HEADS * HEAD_W:].T)


def _cachekv_call(cache_ckv, cache_krope, layer, w_ukv, place):
    B, _, Lc, _ = cache_ckv.shape
    return pl.pallas_call(
        _cachekv_kernel,
        grid=(B,),
        in_specs=[pl.BlockSpec((1, 1, Lc, KV_LORA), lambda b: (b, layer, 0, 0)),
                  pl.BlockSpec((1, 1, Lc, ROPE_DIM), lambda b: (b, layer, 0, 0)),
                  pl.BlockSpec(w_ukv.shape, lambda b: (0, 0)),
                  pl.BlockSpec(place.shape, lambda b: (0, 0))],
        out_specs=[pl.BlockSpec((1, Lc, N_HEADS * HEAD_W), lambda b: (b, 0, 0)),
                   pl.BlockSpec((1, N_HEADS * V_DIM, Lc), lambda b: (b, 0, 0))],
        out_shape=[jax.ShapeDtypeStruct((B, Lc, N_HEADS * HEAD_W), BF16),
                   jax.ShapeDtypeStruct((B, N_HEADS * V_DIM, Lc), BF16)],
        compiler_params=_params(1),
        name="cache_kv",
    )(cache_ckv, cache_krope, w_ukv, place)


def _nt_dot(a, b):
    return lax.dot_general(a, b, (((1,), (1,)), ((), ())), preferred_element_type=F32)


def _attn_kernel(ctx, sub, q_ref, k_ref, v_ref, *rest):
    if ctx:
        kc_ref, vc_ref, o_ref, s_buf = rest
    else:
        o_ref, s_buf = rest
    lk = k_ref.shape[1]
    n_sub = q_ref.shape[1] // sub

    def scores(t):
        q = q_ref[0, t * sub:(t + 1) * sub, :]
        s_buf[t % 2, 0:lk, :] = _nt_dot(k_ref[0], q)
        if ctx:
            s_buf[t % 2, lk:, :] = _nt_dot(kc_ref[0], q)

    scores(0)
    for t in range(n_sub):
        if t + 1 < n_sub:
            scores(t + 1)
        s = s_buf[t % 2]
        m = jnp.max(s, axis=0, keepdims=True)
        e = jnp.exp2(s - m)
        l = jnp.sum(e, axis=0, keepdims=True)
        p = _bf(e)
        o = _dot(v_ref[0], p[0:lk])
        if ctx:
            o = o + _dot(vc_ref[0], p[lk:])
        o_ref[0, t * sub:(t + 1) * sub, :] = _bf((o / l).T)


def _attn_call(q, k, v, kc, vc, tq, sub):
    B, L, _ = q.shape
    ctx = kc is not None
    k_spec = lambda a: pl.BlockSpec((1, a.shape[1], HEAD_W), lambda b, h, i: (b, 0, h))
    v_spec = lambda a: pl.BlockSpec((1, V_DIM, a.shape[2]), lambda b, h, i: (b, h, 0))
    in_specs = [pl.BlockSpec((1, tq, HEAD_W), lambda b, h, i: (b, i, h)), k_spec(k), v_spec(v)]
    args = [q, k, v]
    if ctx:
        in_specs += [k_spec(kc), v_spec(vc)]
        args += [kc, vc]
    return pl.pallas_call(
        functools.partial(_attn_kernel, ctx, sub),
        grid=(B, N_HEADS, L // tq),
        in_specs=in_specs,
        out_specs=pl.BlockSpec((1, tq, V_DIM), lambda b, h, i: (b, i, h)),
        out_shape=jax.ShapeDtypeStruct((B, L, N_HEADS * V_DIM), BF16),
        scratch_shapes=[pltpu.VMEM((2, k.shape[1] + (kc.shape[1] if ctx else 0), sub), F32)],
        compiler_params=_params(3),
        name="attention",
    )(*args)


def _fill_halo(dst_ref, src_ref, fn, i, n_tiles, tm):
    t0 = pl.multiple_of(i * tm, tm)
    w = dst_ref.shape[1]
    dst_ref[HALO:HALO + tm, :] = fn(src_ref[0, pl.ds(t0, tm), :])

    @pl.when(i > 0)
    def _():
        dst_ref[0:HALO, :] = fn(src_ref[0, pl.ds(pl.multiple_of(jnp.maximum(t0 - HALO, 0), HALO), HALO), :])

    @pl.when(i == 0)
    def _():
        dst_ref[0:HALO, :] = jnp.zeros((HALO, w), dst_ref.dtype)

    @pl.when(i < n_tiles - 1)
    def _():
        nxt = jnp.minimum(t0 + tm, (n_tiles - 1) * tm)
        dst_ref[HALO + tm:, :] = fn(src_ref[0, pl.ds(pl.multiple_of(nxt, HALO), HALO), :])

    @pl.when(i == n_tiles - 1)
    def _():
        dst_ref[HALO + tm:, :] = jnp.zeros((HALO, w), dst_ref.dtype)


def _mix_kernel(tm, seq_len, x_ref, zp_ref, zc_ref, att_ref, mod_ref, pw_ref, ps_ref, cw_ref, cb_ref,
                lng_ref, lnb_ref, wout_ref, g2_ref, xo_ref, h2_ref, pbuf, cbuf, cstage, wbuf, ybuf):
    i = pl.program_id(1)
    n_tiles = seq_len // tm
    _fill_halo(pbuf, zp_ref, lambda a: a, i, n_tiles, tm)
    _fill_halo(cbuf, zc_ref, lambda a: a[:, :CONV_W] * jax.nn.sigmoid(a[:, CONV_W:]), i, n_tiles, tm)

    z = pbuf[...]
    n = tm + 2 * HALO
    t = i * tm + lax.broadcasted_iota(jnp.int32, (tm, LANES), 0)
    lane = lax.broadcasted_iota(jnp.int32, (tm, LANES), 1)

    def count(win):
        return (jnp.minimum(t + win // 2, seq_len) - jnp.maximum(t - win // 2, 0)).astype(F32)

    zl, zr = z[:, :LANES], z[:, LANES:]
    a2l = zl[0:n - 1] + zl[1:n]
    a4l = a2l[0:n - 3] + a2l[2:n - 1]
    a2r = zr[0:n - 1] + zr[1:n]
    a4r = a2r[0:n - 3] + a2r[2:n - 1]
    a8r = a4r[0:n - 7] + a4r[4:n - 3]
    a16r = a8r[0:n - 15] + a8r[8:n - 7]
    low = lane < POOL_GW
    pooled_l = jnp.where(low, a2l[HALO - 1:HALO - 1 + tm] / count(2), a4l[HALO - 2:HALO - 2 + tm] / count(4))
    pooled_r = jnp.where(low, a8r[HALO - 4:HALO - 4 + tm] / count(8), a16r[HALO - 8:HALO - 8 + tm] / count(16))
    zc_l, zc_r = zl[HALO:HALO + tm], zr[HALO:HALO + tm]
    d = jnp.concatenate([pooled_l - zc_l, pooled_r - zc_r], axis=1)
    y_pool = _dot(_bf(d), pw_ref[...]) * ps_ref[...]

    seg = tm // SUBLANES
    pitch = seg + 2 * HALO + 4
    n_slabs = CONV_W // LANES
    for s in range(SUBLANES):
        for k in range(n_slabs):
            cstage[k, s * pitch:s * pitch + seg + 2 * HALO, :] = (
                cbuf[s * seg:s * seg + seg + 2 * HALO, k * LANES:(k + 1) * LANES])
    for kk in range(CONV_K):
        wbuf[kk] = jnp.broadcast_to(cw_ref[kk:kk + 1, :], (SUBLANES, CONV_W))
    jb = 16
    first = HALO - CONV_K // 2
    cols = []
    for k in range(n_slabs):
        bias = jnp.broadcast_to(cb_ref[:, k * LANES:(k + 1) * LANES], (SUBLANES, LANES))
        groups = []
        for j0 in range(0, seg, jb):
            accs = [bias] * jb
            for p in range(j0, j0 + jb + CONV_K - 1):
                xrow = cstage[k, pl.ds(first + p, SUBLANES, stride=pitch), :]
                for j in range(max(j0, p - CONV_K + 1), min(j0 + jb - 1, p) + 1):
                    accs[j - j0] = accs[j - j0] + xrow * wbuf[p - j, :, k * LANES:(k + 1) * LANES]
            groups += accs
        cols.append(jnp.concatenate(groups, axis=0))
    u = jnp.concatenate(cols, axis=1)
    mu = jnp.mean(u, axis=-1, keepdims=True)
    var = jnp.mean(jnp.square(u - mu), axis=-1, keepdims=True)
    y_perm = _silu((u - mu) * lax.rsqrt(var + EPS) * lng_ref[...] + lnb_ref[...])
    for k in range(n_slabs):
        ybuf[k] = y_perm[:, k * LANES:(k + 1) * LANES]
    y_conv = jnp.concatenate(
        [jnp.concatenate([ybuf[k, pl.ds(s + SUBLANES * SUBLANES * m, SUBLANES, stride=SUBLANES), :]
                          for s in range(SUBLANES) for m in range(seg // SUBLANES)], axis=0)
         for k in range(n_slabs)], axis=1)

    o1, o2 = POOL_W, POOL_W + MLA_W
    y_mix = (_dot(_bf(y_pool), wout_ref[0:o1, :]) + _dot(att_ref[0], wout_ref[o1:o2, :])
             + _dot(_bf(y_conv), wout_ref[o2:, :]))
    mod = mod_ref[0]
    x = x_ref[0] + mod[2:3] * y_mix
    xo_ref[0] = x
    h2_ref[0] = _bf(_rms(x, g2_ref[...]) * (1.0 + mod[4:5]) + mod[3:4])


def _mix_call(x, zp, zc, att, mod, per_batch_mod, lw, tm):
    B, L, D = x.shape
    full = lambda a: pl.BlockSpec(a.shape, lambda b, i: (0,) * a.ndim)
    tile = lambda w: pl.BlockSpec((1, tm, w), lambda b, i: (b, i, 0))
    seq = lambda w: pl.BlockSpec((1, L, w), lambda b, i: (b, 0, 0))
    mod_map = (lambda b, i: (b, 0, 0)) if per_batch_mod else (lambda b, i: (0, 0, 0))
    weights = [lw["pool_bd"], lw["pool_scale"], lw["conv_w"], lw["conv_b"], lw["ln_g"], lw["ln_b"],
               lw["w_out"], lw["g2"]]
    return pl.pallas_call(
        functools.partial(_mix_kernel, tm, L),
        grid=(B, L // tm),
        in_specs=[tile(D), seq(POOL_W), seq(2 * CONV_W), tile(MLA_W), pl.BlockSpec((1, 8, D), mod_map)]
        + [full(w) for w in weights],
        out_specs=[tile(D), tile(D)],
        out_shape=[jax.ShapeDtypeStruct((B, L, D), F32), jax.ShapeDtypeStruct((B, L, D), BF16)],
        scratch_shapes=[pltpu.VMEM((tm + 2 * HALO, POOL_W), F32), pltpu.VMEM((tm + 2 * HALO, CONV_W), F32),
                        pltpu.VMEM((CONV_W // LANES, SUBLANES * (tm // SUBLANES + 2 * HALO + 4), LANES), F32),
                        pltpu.VMEM((CONV_K + 1, SUBLANES, CONV_W), F32),
                        pltpu.VMEM((CONV_W // LANES, tm, LANES), F32)],
        compiler_params=_params(2),
        name="mixer",
    )(x, zp, zc, att, mod, *weights)


def _ffn_kernel(nq, rows, final, x_ref, h2_ref, mod_ref, wup_ref, cw_ref, cb_ref, wdn_ref, fg_ref,
                o_ref, hbuf, stage, pbuf, abuf, ybuf):
    tm = nq * rows
    seg = tm // SUBLANES
    per_piece = SUBLANES // nq
    pitch = seg + 2 * HALO + 4
    n_slabs = D_MODEL // LANES

    if nq == 1:
        _fill_halo(hbuf, h2_ref, lambda a: a, pl.program_id(1), pl.num_programs(1), tm)
    else:
        for q in range(nq + 1):
            hbuf[q * (rows + HALO):q * (rows + HALO) + HALO, :] = jnp.zeros((HALO, D_MODEL), BF16)
        for q in range(nq):
            hbuf[q * (rows + HALO) + HALO:(q + 1) * (rows + HALO), :] = h2_ref[q]

    for s in range(SUBLANES):
        src = (s // per_piece) * (rows + HALO) + (s % per_piece) * seg
        for k in range(n_slabs):
            stage[k, s * pitch:s * pitch + seg + 2 * HALO, :] = (
                hbuf[src:src + seg + 2 * HALO, k * LANES:(k + 1) * LANES].astype(F32))

    def permute(g, carry):
        for k in range(n_slabs):
            lo = stage[k, pl.ds(HALO - 1 + 2 * g, SUBLANES, stride=pitch), :]
            hi = stage[k, pl.ds(HALO + 2 * g, SUBLANES, stride=pitch), :]
            pbuf[pl.ds(pl.multiple_of(g * 2 * SUBLANES, 2 * SUBLANES), 2 * SUBLANES), k * LANES:(k + 1) * LANES] = (
                _bf(jnp.concatenate([lo, hi], axis=0)))
        return carry

    lax.fori_loop(0, seg // 2 + 1, permute, 0)

    def conv_up(c0):
        u = _dot(pbuf[...], wup_ref[:, c0:c0 + FF_CHUNK])
        cw = cw_ref[:, c0:c0 + FF_CHUNK]
        return (u[0:tm] * cw[0:1] + u[SUBLANES:tm + SUBLANES] * cw[1:2]
                + u[2 * SUBLANES:tm + 2 * SUBLANES] * cw[2:3] + cb_ref[:, c0:c0 + FF_CHUNK])

    for j in range(N_FF_CHUNKS):
        gate_j, value_j = conv_up(j * FF_CHUNK), conv_up(D_FF + j * FF_CHUNK)
        abuf[:, j * FF_CHUNK:(j + 1) * FF_CHUNK] = _bf(_silu(gate_j) * value_j)

    y = _dot(abuf[...], wdn_ref[...])
    for k in range(n_slabs):
        ybuf[k] = y[:, k * LANES:(k + 1) * LANES]
    gate = mod_ref[0][5:6]
    for s in range(SUBLANES):
        q, r0 = s // per_piece, (s % per_piece) * seg
        ys = jnp.concatenate(
            [jnp.concatenate([ybuf[k, pl.ds(s + SUBLANES * SUBLANES * m, SUBLANES, stride=SUBLANES), :]
                              for m in range(seg // SUBLANES)], axis=0) for k in range(n_slabs)], axis=1)
        x = x_ref[q, r0:r0 + seg, :] + gate * ys
        o_ref[q, r0:r0 + seg, :] = _rms(x, fg_ref[...]) if final else x


def _ffn_call(x, h2, mod, per_batch_mod, lw, final_g, final, nq, rows):
    B, L, D = x.shape
    assert (nq == 1 and L % rows == 0) or (rows == L and B % nq == 0)
    tm = nq * rows
    seg = tm // SUBLANES
    once = lambda a: pl.BlockSpec(a.shape, lambda b, i: (0,) * a.ndim, pipeline_mode=pl.Buffered(1))
    tile = pl.BlockSpec((nq, rows, D), lambda b, i: (b, i, 0))
    h2_spec = pl.BlockSpec((nq, L, D), lambda b, i: (b, 0, 0))
    mod_map = (lambda b, i: (b, 0, 0)) if per_batch_mod else (lambda b, i: (0, 0, 0))
    weights = [lw["w_up"], lw["ffn_cw"], lw["ffn_cb"], lw["w_down"], final_g]
    return pl.pallas_call(
        functools.partial(_ffn_kernel, nq, rows, final),
        grid=(B // nq, L // rows),
        in_specs=[tile, h2_spec, pl.BlockSpec((1, 8, D), mod_map)] + [once(w) for w in weights],
        out_specs=tile,
        out_shape=jax.ShapeDtypeStruct((B, L, D), F32),
        scratch_shapes=[pltpu.VMEM((nq * (rows + HALO) + HALO, D), BF16),
                        pltpu.VMEM((D // LANES, SUBLANES * (seg + 2 * HALO + 4), LANES), F32),
                        pltpu.VMEM(((seg + 2) * SUBLANES, D), BF16),
                        pltpu.VMEM((tm, D_FF), BF16),
                        pltpu.VMEM((D // LANES, tm, LANES), F32)],
        compiler_params=_params(2),
        name="conv_ffn",
    )(x, h2, mod, *weights)


def _layout_indices():
    sw = np.arange(ROPE_DIM) ^ (ROPE_DIM // 4)
    o3 = POOL_W + Q_LORA + KV_LORA
    o4 = o3 + ROPE_DIM
    idx_in = np.concatenate([np.arange(0, o3), np.arange(o4, IN_W), np.arange(o3, o4), o3 + sw])
    qh = NOPE_DIM + ROPE_DIM
    half = NOPE_DIM // 2
    idx_uq = np.concatenate([
        np.concatenate([h * qh + NOPE_DIM + np.arange(ROPE_DIM), h * qh + np.arange(half),
                        h * qh + NOPE_DIM + sw, h * qh + half + np.arange(half)])
        for h in range(N_HEADS)])
    kvh = NOPE_DIM + V_DIM
    zero_col = N_HEADS * kvh
    zeros = np.full(half, zero_col)
    idx_k = np.concatenate([
        np.concatenate([zeros, h * kvh + np.arange(half), zeros, h * kvh + half + np.arange(half)])
        for h in range(N_HEADS)])
    idx_v = np.concatenate([h * kvh + NOPE_DIM + np.arange(V_DIM) for h in range(N_HEADS)])
    return idx_in, idx_uq, np.concatenate([idx_k, idx_v])


def _take_cols(w, idx, zero_col):
    parts, start = [], 0
    for i in range(1, len(idx) + 1):
        run_ends = i == len(idx) or (idx[i] != idx[i - 1] + 1 and not (idx[i] == idx[i - 1] == zero_col))
        if run_ends:
            a, n = int(idx[start]), i - start
            parts.append(jnp.zeros((w.shape[0], n), BF16) if a == zero_col else _bf(w[:, a:a + n]))
            start = i
    return jnp.concatenate(parts, axis=1)


def _prep_layer(l, p):
    idx_in, idx_uq, idx_ukv = _layout_indices()
    row = lambda a: a[l][None, :]
    eye = jnp.eye(POOL_GROUPS, dtype=F32)
    pool_bd = (eye[:, None, :, None] * p["pool_w"][l][:, :, None, :]).reshape(POOL_W, POOL_W)
    return {
        "g1": row(p["norm1_g"]),
        "w_in": _take_cols(p["w_in"][l], idx_in, -1),
        "qg": row(p["q_norm_g"]),
        "w_uq": _take_cols(p["w_uq"][l], idx_uq, -1),
        "kvg": row(p["kv_norm_g"]),
        "w_ukv": _take_cols(p["w_ukv"][l], idx_ukv, N_HEADS * (NOPE_DIM + V_DIM)),
        "pool_bd": _bf(pool_bd),
        "pool_scale": row(p["pool_scale"]),
        "conv_w": jnp.pad(p["conv_w"][l], ((0, 1), (0, 0))),
        "conv_b": row(p["conv_b"]),
        "ln_g": row(p["conv_ln_g"]),
        "ln_b": row(p["conv_ln_b"]),
        "w_out": _bf(p["w_out"][l]),
        "g2": row(p["norm2_g"]),
        "w_up": _bf(p["w_up"][l]),
        "ffn_cw": jnp.pad(p["ffn_conv_w"][l], ((0, SUBLANES - FFN_K), (0, 0))),
        "ffn_cb": row(p["ffn_conv_b"]),
        "w_down": _bf(p["w_down"][l]),
    }


def _rope_table(L):
    rows = L // GRID_W
    row = jnp.repeat(jnp.arange(rows, dtype=F32), GRID_W)
    col = jnp.tile(jnp.arange(GRID_W, dtype=F32), rows)
    n = ROPE_DIM // 4
    inv = ROPE_THETA ** (-jnp.arange(n, dtype=F32) / n)
    ang_r, ang_c = row[:, None] * inv, col[:, None] * inv
    cos = jnp.concatenate([jnp.cos(ang_r)] * 2 + [jnp.cos(ang_c)] * 2, axis=1)
    sin = jnp.concatenate([-jnp.sin(ang_r), jnp.sin(ang_r), -jnp.sin(ang_c), jnp.sin(ang_c)], axis=1)
    one, zero = jnp.ones_like(cos), jnp.zeros_like(cos)
    return jnp.concatenate([cos, one, sin, zero, cos, zero], axis=1)


def kernel(x_prompt, x_sample, cache_ckv, cache_krope, c, c_ctx, ada_w, ada_b, norm1_g, w_in, pool_w,
           pool_scale, q_norm_g, w_uq, kv_norm_g, w_ukv, conv_w, conv_b, conv_ln_g, conv_ln_b, w_out,
           norm2_g, w_up, ffn_conv_w, ffn_conv_b, w_down, final_g):
    p = dict(norm1_g=norm1_g, w_in=w_in, pool_w=pool_w, pool_scale=pool_scale, q_norm_g=q_norm_g,
             w_uq=w_uq, kv_norm_g=kv_norm_g, w_ukv=w_ukv, conv_w=conv_w, conv_b=conv_b,
             conv_ln_g=conv_ln_g, conv_ln_b=conv_ln_b, w_out=w_out, norm2_g=norm2_g, w_up=w_up,
             ffn_conv_w=ffn_conv_w, ffn_conv_b=ffn_conv_b, w_down=w_down)
    n_dec = c.shape[0]
    c_all = jnp.concatenate([c, c_ctx[None, :], jnp.zeros((16 - n_dec - 1, D_MODEL), F32)], axis=0)
    mods = _mod_call(c_all, ada_w, ada_b.reshape(DEPTH, 1, 6 * D_MODEL))
    mods = jnp.pad(mods.reshape(DEPTH, 16, 6, D_MODEL), ((0, 0), (0, 0), (0, 2), (0, 0)))
    layers = [_prep_layer(l, p) for l in range(DEPTH)]
    fg = final_g[None, :]
    place = _bf(jnp.eye(ROPE_DIM, LANES, dtype=F32))
    rope_tab = _rope_table(x_sample.shape[1])

    def run(x, mod_of, per_batch_mod, rope, cache, tiles):
        states = []
        for l in range(DEPTH):
            lw, mod = layers[l], mod_of(l)
            outs = _inproj_call(x, mod, per_batch_mod, lw, rope, cache is None, tiles["inproj"])
            zp, zc, q, k, v = outs[:5]
            if cache is None:
                states.append(outs[5:])
                att = _attn_call(q, k, v, None, None, *tiles["attn"])
            else:
                kc, vc = _cachekv_call(cache[0], cache[1], l, lw["w_ukv"], place)
                att = _attn_call(q, k, v, kc, vc, *tiles["attn"])
            x, h2 = _mix_call(x, zp, zc, att, mod, per_batch_mod, lw, tiles["mix"])
            x = _ffn_call(x, h2, mod, per_batch_mod, lw, fg, l == DEPTH - 1, *tiles["ffn"])
        return x, states

    y_prompt, states = run(x_prompt, lambda l: mods[l, n_dec:n_dec + 1], False, None, None, CTX_TILES)
    y_sample, _ = run(x_sample, lambda l: mods[l, :n_dec], True, rope_tab, (cache_ckv, cache_krope), DEC_TILES)
    state_ckv = jnp.stack([s[0] for s in states], axis=1)
    state_krope = jnp.stack([s[1] for s in states], axis=1)
    return (y_prompt, y_sample, state_ckv, state_krope)
```

```python
import functools
import math

import numpy as np
import jax
import jax.numpy as jnp
from jax import lax
from jax.experimental import pallas as pl
from jax.experimental.pallas import tpu as pltpu

D_MODEL = 1024
DEPTH = 4
GRID_W = 64
POOL_W = 256
POOL_GROUPS = 4
POOL_GW = 64
POOL_WINDOWS = (2, 4, 8, 16)
CONV_W = 256
CONV_K = 31
MLA_W = 512
N_HEADS = 4
V_DIM = 128
NOPE_DIM = 128
ROPE_DIM = 64
Q_LORA = 384
KV_LORA = 256
IN_W = POOL_W + Q_LORA + KV_LORA + ROPE_DIM + 2 * CONV_W
D_FF = 2816
FFN_K = 3
ROPE_THETA = 10000.0
EPS = 1e-6

LANES = 128
SUBLANES = 8
HEAD_W = 2 * LANES
IN_W_PAD = IN_W + ROPE_DIM
HALO = 16
FF_CHUNK = 256
KEY_CHUNK = 512
N_FF_CHUNKS = D_FF // FF_CHUNK
VMEM_LIMIT = 56 * 1024 * 1024

CTX_TILES = {"inproj": 256, "attn": (256, 256, 4), "mix": 256, "ffn": (2, 256)}
DEC_TILES = {"inproj": 512, "attn": (2048, 512, 1), "mix": 512, "ffn": (1, 512)}

F32 = jnp.float32
BF16 = jnp.bfloat16


def _bf(x):
    return x.astype(BF16)


def _dot(a, b):
    return jnp.dot(a, b, preferred_element_type=F32)


def _rms(x, g):
    return x * lax.rsqrt(jnp.mean(x * x, axis=-1, keepdims=True) + EPS) * g


def _silu(x):
    return x * jax.nn.sigmoid(x)


def _params(n_parallel):
    return pltpu.CompilerParams(dimension_semantics=("parallel",) * n_parallel,
                                vmem_limit_bytes=VMEM_LIMIT)


def _mod_kernel(c_ref, w_ref, b_ref, o_ref):
    o_ref[0] = _dot(_bf(_silu(c_ref[...])), _bf(w_ref[0])) + b_ref[0]


def _mod_call(c_all, ada_w, ada_b):
    R = c_all.shape[0]
    tn = 1536
    return pl.pallas_call(
        _mod_kernel,
        grid=(DEPTH, 6 * D_MODEL // tn),
        in_specs=[pl.BlockSpec((R, D_MODEL), lambda l, j: (0, 0)),
                  pl.BlockSpec((1, D_MODEL, tn), lambda l, j: (l, 0, j)),
                  pl.BlockSpec((1, 1, tn), lambda l, j: (l, 0, j))],
        out_specs=pl.BlockSpec((1, R, tn), lambda l, j: (l, 0, j)),
        out_shape=jax.ShapeDtypeStruct((DEPTH, R, 6 * D_MODEL), F32),
        compiler_params=_params(2),
        name="adaln_mod",
    )(c_all, ada_w, ada_b)


def _inproj_kernel(rope, state, x_ref, mod_ref, g1_ref, win_ref, qg_ref, wuq_ref, kvg_ref, wukv_ref,
                   *rest):
    if rope:
        rope_ref, rest = rest[0], rest[1:]
    zp_ref, zc_ref, q_ref, k_ref, v_ref = rest[:5]
    x = x_ref[0]
    mod = mod_ref[0]
    h = _rms(x, g1_ref[...]) * (1.0 + mod[1:2]) + mod[0:1]
    z = _dot(_bf(h), win_ref[...])
    o1 = POOL_W
    o2 = o1 + Q_LORA
    o3 = o2 + KV_LORA
    o4 = o3 + 2 * CONV_W
    zp_ref[0] = z[:, :o1]
    zc_ref[0] = z[:, o3:o3 + CONV_W] * jax.nn.sigmoid(z[:, o3 + CONV_W:o4])
    kr = z[:, o4:]
    scale = math.log2(math.e) / math.sqrt(NOPE_DIM + ROPE_DIM)
    qn = _rms(z[:, o1:o2], qg_ref[...] * scale)
    q = _dot(_bf(qn), wuq_ref[...])
    ckv = _rms(z[:, o2:o3], kvg_ref[...])
    kv = _dot(_bf(ckv), wukv_ref[...])
    lane = lax.broadcasted_iota(jnp.int32, kr.shape, 1)
    if rope:
        tab = rope_ref[...]
        qa, qb, ka = tab[:, :LANES], tab[:, LANES:2 * LANES], tab[:, 2 * LANES:]
        krot = kr * ka + pltpu.roll(kr, ROPE_DIM, 1) * qb
    else:
        krot = jnp.where(lane < ROPE_DIM, kr, 0.0)
    for hd in range(N_HEADS):
        c0 = hd * HEAD_W
        q0 = q[:, c0:c0 + LANES]
        q1 = q[:, c0 + LANES:c0 + HEAD_W]
        if rope:
            q0 = q0 * qa + q1 * qb
        q_ref[0, :, c0:c0 + LANES] = _bf(q0)
        q_ref[0, :, c0 + LANES:c0 + HEAD_W] = _bf(q1)
        k_ref[0, :, c0:c0 + LANES] = _bf(kv[:, c0:c0 + LANES] + krot)
        k_ref[0, :, c0 + LANES:c0 + HEAD_W] = _bf(kv[:, c0 + LANES:c0 + HEAD_W])
    v_ref[0] = _bf(kv[:, N_HEADS * HEAD_W:].T)
    if state:
        ckv_ref, kr_ref = rest[5:7]
        ckv_ref[0] = ckv
        kr_ref[0] = kr[:, :ROPE_DIM]


def _inproj_call(x, mod, per_batch_mod, lw, rope_tab, state, tm):
    B, L, D = x.shape
    rope = rope_tab is not None
    full = lambda a: pl.BlockSpec(a.shape, lambda b, i: (0,) * a.ndim)
    tile = lambda w: pl.BlockSpec((1, tm, w), lambda b, i: (b, i, 0))
    mod_map = (lambda b, i: (b, 0, 0)) if per_batch_mod else (lambda b, i: (0, 0, 0))
    weights = [lw["g1"], lw["w_in"], lw["qg"], lw["w_uq"], lw["kvg"], lw["w_ukv"]]
    in_specs = [tile(D), pl.BlockSpec((1, 8, D), mod_map)] + [full(w) for w in weights]
    args = [x, mod] + weights
    if rope:
        in_specs.append(pl.BlockSpec((tm, 3 * LANES), lambda b, i: (i, 0)))
        args.append(rope_tab)
    widths = [(POOL_W, F32), (CONV_W, F32), (N_HEADS * HEAD_W, BF16), (N_HEADS * HEAD_W, BF16)]
    states = [(KV_LORA, F32), (ROPE_DIM, F32)] if state else []
    vt_spec = pl.BlockSpec((1, N_HEADS * V_DIM, tm), lambda b, i: (b, 0, i))
    vt_shape = jax.ShapeDtypeStruct((B, N_HEADS * V_DIM, L), BF16)
    return pl.pallas_call(
        functools.partial(_inproj_kernel, rope, state),
        grid=(B, L // tm),
        in_specs=in_specs,
        out_specs=[tile(w) for w, _ in widths] + [vt_spec] + [tile(w) for w, _ in states],
        out_shape=[jax.ShapeDtypeStruct((B, L, w), dt) for w, dt in widths] + [vt_shape]
        + [jax.ShapeDtypeStruct((B, L, w), dt) for w, dt in states],
        compiler_params=_params(2),
        name="inproj",
    )(*args)


def _cachekv_kernel(ckv_ref, kr_ref, wukv_ref, place_ref, k_ref, v_ref):
    kv = _dot(_bf(ckv_ref[0, 0]), wukv_ref[...])
    krot = _dot(_bf(kr_ref[0, 0]), place_ref[...])
    for hd in range(N_HEADS):
        c0 = hd * HEAD_W
        k_ref[0, :, c0:c0 + LANES] = _bf(kv[:, c0:c0 + LANES] + krot)
        k_ref[0, :, c0 + LANES:c0 + HEAD_W] = _bf(kv[:, c0 + LANES:c0 + HEAD_W])
    v_ref[0] = _bf(kv[:, N_HEADS * HEAD_W:].T)


def _cachekv_call(cache_ckv, cache_krope, layer, w_ukv, place):
    B, _, Lc, _ = cache_ckv.shape
    return pl.pallas_call(
        _cachekv_kernel,
        grid=(B,),
        in_specs=[pl.BlockSpec((1, 1, Lc, KV_LORA), lambda b: (b, layer, 0, 0)),
                  pl.BlockSpec((1, 1, Lc, ROPE_DIM), lambda b: (b, layer, 0, 0)),
                  pl.BlockSpec(w_ukv.shape, lambda b: (0, 0)),
                  pl.BlockSpec(place.shape, lambda b: (0, 0))],
        out_specs=[pl.BlockSpec((1, Lc, N_HEADS * HEAD_W), lambda b: (b, 0, 0)),
                   pl.BlockSpec((1, N_HEADS * V_DIM, Lc), lambda b: (b, 0, 0))],
        out_shape=[jax.ShapeDtypeStruct((B, Lc, N_HEADS * HEAD_W), BF16),
                   jax.ShapeDtypeStruct((B, N_HEADS * V_DIM, Lc), BF16)],
        compiler_params=_params(1),
        name="cache_kv",
    )(cache_ckv, cache_krope, w_ukv, place)


def _nt_dot(a, b):
    return lax.dot_general(a, b, (((1,), (1,)), ((), ())), preferred_element_type=F32)


def _attn_kernel(ctx, sub, heads, q_ref, k_ref, v_ref, *rest):
    if ctx:
        kc_ref, vc_ref, o_ref, s_buf = rest
    else:
        o_ref, s_buf = rest
    lk = k_ref.shape[1]
    ch = min(KEY_CHUNK, lk)
    chunks = [(k_ref, v_ref, r, r) for r in range(0, lk, ch)]
    if ctx:
        chunks += [(kc_ref, vc_ref, r, lk + r) for r in range(0, kc_ref.shape[1], ch)]
    items = [(hd, t) for hd in range(heads) for t in range(q_ref.shape[1] // sub)]

    def scores(i, c):
        hd, t = items[i]
        keys, _, r, row = chunks[c]
        s_buf[i % 2, row:row + ch, :] = _nt_dot(keys[0, r:r + ch, hd * HEAD_W:(hd + 1) * HEAD_W],
                                                q_ref[0, t * sub:(t + 1) * sub, hd * HEAD_W:(hd + 1) * HEAD_W])

    for c in range(len(chunks)):
        scores(0, c)
    for i, (hd, t) in enumerate(items):
        m = jnp.max(s_buf[i % 2], axis=0, keepdims=True)
        l = jnp.zeros((1, sub), F32)
        o = jnp.zeros((V_DIM, sub), F32)
        for c, (_, values, r, row) in enumerate(chunks):
            if i + 1 < len(items):
                scores(i + 1, c)
            e = jnp.exp2(s_buf[i % 2, row:row + ch, :] - m)
            l = l + jnp.sum(e, axis=0, keepdims=True)
            o = o + _dot(values[0, hd * V_DIM:(hd + 1) * V_DIM, r:r + ch], _bf(e))
        o_ref[0, t * sub:(t + 1) * sub, hd * V_DIM:(hd + 1) * V_DIM] = _bf((o / l).T)


def _attn_call(q, k, v, kc, vc, tq, sub, heads):
    B, L, _ = q.shape
    ctx = kc is not None
    k_spec = lambda a: pl.BlockSpec((1, a.shape[1], heads * HEAD_W), lambda b, h, i: (b, 0, h))
    v_spec = lambda a: pl.BlockSpec((1, heads * V_DIM, a.shape[2]), lambda b, h, i: (b, h, 0))
    in_specs = [pl.BlockSpec((1, tq, heads * HEAD_W), lambda b, h, i: (b, i, h)), k_spec(k), v_spec(v)]
    args = [q, k, v]
    if ctx:
        in_specs += [k_spec(kc), v_spec(vc)]
        args += [kc, vc]
    return pl.pallas_call(
        functools.partial(_attn_kernel, ctx, sub, heads),
        grid=(B, N_HEADS // heads, L // tq),
        in_specs=in_specs,
        out_specs=pl.BlockSpec((1, tq, heads * V_DIM), lambda b, h, i: (b, i, h)),
        out_shape=jax.ShapeDtypeStruct((B, L, N_HEADS * V_DIM), BF16),
        scratch_shapes=[pltpu.VMEM((2, k.shape[1] + (kc.shape[1] if ctx else 0), sub), F32)],
        compiler_params=_params(3),
        name="attention",
    )(*args)


def _fill_halo(dst_ref, src_ref, fn, i, n_tiles, tm):
    t0 = pl.multiple_of(i * tm, tm)
    w = dst_ref.shape[1]
    dst_ref[HALO:HALO + tm, :] = fn(src_ref[0, pl.ds(t0, tm), :])

    @pl.when(i > 0)
    def _():
        dst_ref[0:HALO, :] = fn(src_ref[0, pl.ds(pl.multiple_of(jnp.maximum(t0 - HALO, 0), HALO), HALO), :])

    @pl.when(i == 0)
    def _():
        dst_ref[0:HALO, :] = jnp.zeros((HALO, w), dst_ref.dtype)

    @pl.when(i < n_tiles - 1)
    def _():
        nxt = jnp.minimum(t0 + tm, (n_tiles - 1) * tm)
        dst_ref[HALO + tm:, :] = fn(src_ref[0, pl.ds(pl.multiple_of(nxt, HALO), HALO), :])

    @pl.when(i == n_tiles - 1)
    def _():
        dst_ref[HALO + tm:, :] = jnp.zeros((HALO, w), dst_ref.dtype)


def _mix_kernel(tm, seq_len, x_ref, zp_ref, zc_ref, att_ref, mod_ref, pw_ref, ps_ref, cw_ref, cb_ref,
                lng_ref, lnb_ref, wout_ref, g2_ref, xo_ref, h2_ref, pbuf, cbuf, cstage, wbuf, ybuf):
    i = pl.program_id(1)
    n_tiles = seq_len // tm
    _fill_halo(pbuf, zp_ref, lambda a: a, i, n_tiles, tm)
    _fill_halo(cbuf, zc_ref, lambda a: a, i, n_tiles, tm)

    z = pbuf[...]
    n = tm + 2 * HALO
    t = i * tm + lax.broadcasted_iota(jnp.int32, (tm, LANES), 0)
    lane = lax.broadcasted_iota(jnp.int32, (tm, LANES), 1)

    def count(win):
        return (jnp.minimum(t + win // 2, seq_len) - jnp.maximum(t - win // 2, 0)).astype(F32)

    zl, zr = z[:, :LANES], z[:, LANES:]
    a2l = zl[0:n - 1] + zl[1:n]
    a4l = a2l[0:n - 3] + a2l[2:n - 1]
    a2r = zr[0:n - 1] + zr[1:n]
    a4r = a2r[0:n - 3] + a2r[2:n - 1]
    a8r = a4r[0:n - 7] + a4r[4:n - 3]
    a16r = a8r[0:n - 15] + a8r[8:n - 7]
    low = lane < POOL_GW
    pooled_l = jnp.where(low, a2l[HALO - 1:HALO - 1 + tm] / count(2), a4l[HALO - 2:HALO - 2 + tm] / count(4))
    pooled_r = jnp.where(low, a8r[HALO - 4:HALO - 4 + tm] / count(8), a16r[HALO - 8:HALO - 8 + tm] / count(16))
    zc_l, zc_r = zl[HALO:HALO + tm], zr[HALO:HALO + tm]
    d = jnp.concatenate([pooled_l - zc_l, pooled_r - zc_r], axis=1)
    y_pool = _dot(_bf(d), pw_ref[...]) * ps_ref[...]

    seg = tm // SUBLANES
    pitch = seg + 2 * HALO + 4
    n_slabs = CONV_W // LANES
    for s in range(SUBLANES):
        for k in range(n_slabs):
            cstage[k, s * pitch:s * pitch + seg + 2 * HALO, :] = (
                cbuf[s * seg:s * seg + seg + 2 * HALO, k * LANES:(k + 1) * LANES])
    for kk in range(CONV_K):
        wbuf[kk] = jnp.broadcast_to(cw_ref[kk:kk + 1, :], (SUBLANES, CONV_W))
    jb = 16
    first = HALO - CONV_K // 2
    cols = []
    for k in range(n_slabs):
        bias = jnp.broadcast_to(cb_ref[:, k * LANES:(k + 1) * LANES], (SUBLANES, LANES))
        groups = []
        for j0 in range(0, seg, jb):
            accs = [bias] * jb
            for p in range(j0, j0 + jb + CONV_K - 1):
                xrow = cstage[k, pl.ds(first + p, SUBLANES, stride=pitch), :]
                for j in range(max(j0, p - CONV_K + 1), min(j0 + jb - 1, p) + 1):
                    accs[j - j0] = accs[j - j0] + xrow * wbuf[p - j, :, k * LANES:(k + 1) * LANES]
            groups += accs
        cols.append(jnp.concatenate(groups, axis=0))
    u = jnp.concatenate(cols, axis=1)
    mu = jnp.mean(u, axis=-1, keepdims=True)
    var = jnp.mean(jnp.square(u - mu), axis=-1, keepdims=True)
    y_perm = _silu((u - mu) * lax.rsqrt(var + EPS) * lng_ref[...] + lnb_ref[...])
    for k in range(n_slabs):
        ybuf[k] = y_perm[:, k * LANES:(k + 1) * LANES]
    y_conv = jnp.concatenate(
        [jnp.concatenate([ybuf[k, pl.ds(s + SUBLANES * SUBLANES * m, SUBLANES, stride=SUBLANES), :]
                          for s in range(SUBLANES) for m in range(seg // SUBLANES)], axis=0)
         for k in range(n_slabs)], axis=1)

    o1, o2 = POOL_W, POOL_W + MLA_W
    y_mix = (_dot(_bf(y_pool), wout_ref[0:o1, :]) + _dot(att_ref[0], wout_ref[o1:o2, :])
             + _dot(_bf(y_conv), wout_ref[o2:, :]))
    mod = mod_ref[0]
    x = x_ref[0] + mod[2:3] * y_mix
    xo_ref[0] = x
    h2_ref[0] = _bf(_rms(x, g2_ref[...]) * (1.0 + mod[4:5]) + mod[3:4])


def _mix_call(x, zp, zc, att, mod, per_batch_mod, lw, tm):
    B, L, D = x.shape
    full = lambda a: pl.BlockSpec(a.shape, lambda b, i: (0,) * a.ndim)
    tile = lambda w: pl.BlockSpec((1, tm, w), lambda b, i: (b, i, 0))
    seq = lambda w: pl.BlockSpec((1, L, w), lambda b, i: (b, 0, 0))
    mod_map = (lambda b, i: (b, 0, 0)) if per_batch_mod else (lambda b, i: (0, 0, 0))
    weights = [lw["pool_bd"], lw["pool_scale"], lw["conv_w"], lw["conv_b"], lw["ln_g"], lw["ln_b"],
               lw["w_out"], lw["g2"]]
    return pl.pallas_call(
        functools.partial(_mix_kernel, tm, L),
        grid=(B, L // tm),
        in_specs=[tile(D), seq(POOL_W), seq(CONV_W), tile(MLA_W), pl.BlockSpec((1, 8, D), mod_map)]
        + [full(w) for w in weights],
        out_specs=[tile(D), tile(D)],
        out_shape=[jax.ShapeDtypeStruct((B, L, D), F32), jax.ShapeDtypeStruct((B, L, D), BF16)],
        scratch_shapes=[pltpu.VMEM((tm + 2 * HALO, POOL_W), F32), pltpu.VMEM((tm + 2 * HALO, CONV_W), F32),
                        pltpu.VMEM((CONV_W // LANES, SUBLANES * (tm // SUBLANES + 2 * HALO + 4), LANES), F32),
                        pltpu.VMEM((CONV_K + 1, SUBLANES, CONV_W), F32),
                        pltpu.VMEM((CONV_W // LANES, tm, LANES), F32)],
        compiler_params=_params(2),
        name="mixer",
    )(x, zp, zc, att, mod, *weights)


def _ffn_kernel(nq, rows, final, x_ref, h2_ref, mod_ref, wup_ref, cw_ref, cb_ref, wdn_ref, fg_ref,
                o_ref, hbuf, stage, pbuf, abuf, ybuf):
    tm = nq * rows
    seg = tm // SUBLANES
    per_piece = SUBLANES // nq
    pitch = seg + 2 * HALO + 4
    n_slabs = D_MODEL // LANES

    if nq == 1:
        _fill_halo(hbuf, h2_ref, lambda a: a, pl.program_id(1), pl.num_programs(1), tm)
    else:
        for q in range(nq + 1):
            hbuf[q * (rows + HALO):q * (rows + HALO) + HALO, :] = jnp.zeros((HALO, D_MODEL), BF16)
        for q in range(nq):
            hbuf[q * (rows + HALO) + HALO:(q + 1) * (rows + HALO), :] = h2_ref[q]

    for s in range(SUBLANES):
        src = (s // per_piece) * (rows + HALO) + (s % per_piece) * seg
        for k in range(n_slabs):
            stage[k, s * pitch:s * pitch + seg + 2 * HALO, :] = (
                hbuf[src:src + seg + 2 * HALO, k * LANES:(k + 1) * LANES].astype(F32))

    def permute(g, carry):
        for k in range(n_slabs):
            lo = stage[k, pl.ds(HALO - 1 + 2 * g, SUBLANES, stride=pitch), :]
            hi = stage[k, pl.ds(HALO + 2 * g, SUBLANES, stride=pitch), :]
            pbuf[pl.ds(pl.multiple_of(g * 2 * SUBLANES, 2 * SUBLANES), 2 * SUBLANES), k * LANES:(k + 1) * LANES] = (
                _bf(jnp.concatenate([lo, hi], axis=0)))
        return carry

    lax.fori_loop(0, seg // 2 + 1, permute, 0)

    def conv_up(c0):
        u = _dot(pbuf[...], wup_ref[:, c0:c0 + FF_CHUNK])
        cw = cw_ref[:, c0:c0 + FF_CHUNK]
        return (u[0:tm] * cw[0:1] + u[SUBLANES:tm + SUBLANES] * cw[1:2]
                + u[2 * SUBLANES:tm + 2 * SUBLANES] * cw[2:3] + cb_ref[:, c0:c0 + FF_CHUNK])

    for j in range(N_FF_CHUNKS):
        gate_j, value_j = conv_up(j * FF_CHUNK), conv_up(D_FF + j * FF_CHUNK)
        abuf[:, j * FF_CHUNK:(j + 1) * FF_CHUNK] = _bf(_silu(gate_j) * value_j)

    y = _dot(abuf[...], wdn_ref[...])
    for k in range(n_slabs):
        ybuf[k] = y[:, k * LANES:(k + 1) * LANES]
    gate = mod_ref[0][5:6]
    for s in range(SUBLANES):
        q, r0 = s // per_piece, (s % per_piece) * seg
        ys = jnp.concatenate(
            [jnp.concatenate([ybuf[k, pl.ds(s + SUBLANES * SUBLANES * m, SUBLANES, stride=SUBLANES), :]
                              for m in range(seg // SUBLANES)], axis=0) for k in range(n_slabs)], axis=1)
        x = x_ref[q, r0:r0 + seg, :] + gate * ys
        o_ref[q, r0:r0 + seg, :] = _rms(x, fg_ref[...]) if final else x


def _ffn_call(x, h2, mod, per_batch_mod, lw, final_g, final, nq, rows):
    B, L, D = x.shape
    assert (nq == 1 and L % rows == 0) or (rows == L and B % nq == 0)
    tm = nq * rows
    seg = tm // SUBLANES
    once = lambda a: pl.BlockSpec(a.shape, lambda b, i: (0,) * a.ndim, pipeline_mode=pl.Buffered(1))
    tile = pl.BlockSpec((nq, rows, D), lambda b, i: (b, i, 0))
    h2_spec = pl.BlockSpec((nq, L, D), lambda b, i: (b, 0, 0))
    mod_map = (lambda b, i: (b, 0, 0)) if per_batch_mod else (lambda b, i: (0, 0, 0))
    weights = [lw["w_up"], lw["ffn_cw"], lw["ffn_cb"], lw["w_down"], final_g]
    return pl.pallas_call(
        functools.partial(_ffn_kernel, nq, rows, final),
        grid=(B // nq, L // rows),
        in_specs=[tile, h2_spec, pl.BlockSpec((1, 8, D), mod_map)] + [once(w) for w in weights],
        out_specs=tile,
        out_shape=jax.ShapeDtypeStruct((B, L, D), F32),
        scratch_shapes=[pltpu.VMEM((nq * (rows + HALO) + HALO, D), BF16),
                        pltpu.VMEM((D // LANES, SUBLANES * (seg + 2 * HALO + 4), LANES), F32),
                        pltpu.VMEM(((seg + 2) * SUBLANES, D), BF16),
                        pltpu.VMEM((tm, D_FF), BF16),
                        pltpu.VMEM((D // LANES, tm, LANES), F32)],
        compiler_params=_params(2),
        name="conv_ffn",
    )(x, h2, mod, *weights)


def _layout_indices():
    sw = np.arange(ROPE_DIM) ^ (ROPE_DIM // 4)
    o3 = POOL_W + Q_LORA + KV_LORA
    o4 = o3 + ROPE_DIM
    idx_in = np.concatenate([np.arange(0, o3), np.arange(o4, IN_W), np.arange(o3, o4), o3 + sw])
    qh = NOPE_DIM + ROPE_DIM
    half = NOPE_DIM // 2
    idx_uq = np.concatenate([
        np.concatenate([h * qh + NOPE_DIM + np.arange(ROPE_DIM), h * qh + np.arange(half),
                        h * qh + NOPE_DIM + sw, h * qh + half + np.arange(half)])
        for h in range(N_HEADS)])
    kvh = NOPE_DIM + V_DIM
    zero_col = N_HEADS * kvh
    zeros = np.full(half, zero_col)
    idx_k = np.concatenate([
        np.concatenate([zeros, h * kvh + np.arange(half), zeros, h * kvh + half + np.arange(half)])
        for h in range(N_HEADS)])
    idx_v = np.concatenate([h * kvh + NOPE_DIM + np.arange(V_DIM) for h in range(N_HEADS)])
    return idx_in, idx_uq, np.concatenate([idx_k, idx_v])


def _take_cols(w, idx, zero_col):
    parts, start = [], 0
    for i in range(1, len(idx) + 1):
        run_ends = i == len(idx) or (idx[i] != idx[i - 1] + 1 and not (idx[i] == idx[i - 1] == zero_col))
        if run_ends:
            a, n = int(idx[start]), i - start
            parts.append(jnp.zeros((w.shape[0], n), BF16) if a == zero_col else _bf(w[:, a:a + n]))
            start = i
    return jnp.concatenate(parts, axis=1)


def _prep_layer(l, p):
    idx_in, idx_uq, idx_ukv = _layout_indices()
    row = lambda a: a[l][None, :]
    eye = jnp.eye(POOL_GROUPS, dtype=F32)
    pool_bd = (eye[:, None, :, None] * p["pool_w"][l][:, :, None, :]).reshape(POOL_W, POOL_W)
    return {
        "g1": row(p["norm1_g"]),
        "w_in": _take_cols(p["w_in"][l], idx_in, -1),
        "qg": row(p["q_norm_g"]),
        "w_uq": _take_cols(p["w_uq"][l], idx_uq, -1),
        "kvg": row(p["kv_norm_g"]),
        "w_ukv": _take_cols(p["w_ukv"][l], idx_ukv, N_HEADS * (NOPE_DIM + V_DIM)),
        "pool_bd": _bf(pool_bd),
        "pool_scale": row(p["pool_scale"]),
        "conv_w": jnp.pad(p["conv_w"][l], ((0, 1), (0, 0))),
        "conv_b": row(p["conv_b"]),
        "ln_g": row(p["conv_ln_g"]),
        "ln_b": row(p["conv_ln_b"]),
        "w_out": _bf(p["w_out"][l]),
        "g2": row(p["norm2_g"]),
        "w_up": _bf(p["w_up"][l]),
        "ffn_cw": jnp.pad(p["ffn_conv_w"][l], ((0, SUBLANES - FFN_K), (0, 0))),
        "ffn_cb": row(p["ffn_conv_b"]),
        "w_down": _bf(p["w_down"][l]),
    }


def _rope_table(L):
    rows = L // GRID_W
    row = jnp.repeat(jnp.arange(rows, dtype=F32), GRID_W)
    col = jnp.tile(jnp.arange(GRID_W, dtype=F32), rows)
    n = ROPE_DIM // 4
    inv = ROPE_THETA ** (-jnp.arange(n, dtype=F32) / n)
    ang_r, ang_c = row[:, None] * inv, col[:, None] * inv
    cos = jnp.concatenate([jnp.cos(ang_r)] * 2 + [jnp.cos(ang_c)] * 2, axis=1)
    sin = jnp.concatenate([-jnp.sin(ang_r), jnp.sin(ang_r), -jnp.sin(ang_c), jnp.sin(ang_c)], axis=1)
    one, zero = jnp.ones_like(cos), jnp.zeros_like(cos)
    return jnp.concatenate([cos, one, sin, zero, cos, zero], axis=1)


def kernel(x_prompt, x_sample, cache_ckv, cache_krope, c, c_ctx, ada_w, ada_b, norm1_g, w_in, pool_w,
           pool_scale, q_norm_g, w_uq, kv_norm_g, w_ukv, conv_w, conv_b, conv_ln_g, conv_ln_b, w_out,
           norm2_g, w_up, ffn_conv_w, ffn_conv_b, w_down, final_g):
    p = dict(norm1_g=norm1_g, w_in=w_in, pool_w=pool_w, pool_scale=pool_scale, q_norm_g=q_norm_g,
             w_uq=w_uq, kv_norm_g=kv_norm_g, w_ukv=w_ukv, conv_w=conv_w, conv_b=conv_b,
             conv_ln_g=conv_ln_g, conv_ln_b=conv_ln_b, w_out=w_out, norm2_g=norm2_g, w_up=w_up,
             ffn_conv_w=ffn_conv_w, ffn_conv_b=ffn_conv_b, w_down=w_down)
    n_dec = c.shape[0]
    c_all = jnp.concatenate([c, c_ctx[None, :], jnp.zeros((16 - n_dec - 1, D_MODEL), F32)], axis=0)
    mods = _mod_call(c_all, ada_w, ada_b.reshape(DEPTH, 1, 6 * D_MODEL))
    mods = jnp.pad(mods.reshape(DEPTH, 16, 6, D_MODEL), ((0, 0), (0, 0), (0, 2), (0, 0)))
    layers = [_prep_layer(l, p) for l in range(DEPTH)]
    fg = final_g[None, :]
    place = _bf(jnp.eye(ROPE_DIM, LANES, dtype=F32))
    rope_tab = _rope_table(x_sample.shape[1])

    def run(x, mod_of, per_batch_mod, rope, cache, tiles):
        states = []
        for l in range(DEPTH):
            lw, mod = layers[l], mod_of(l)
            outs = _inproj_call(x, mod, per_batch_mod, lw, rope, cache is None, tiles["inproj"])
            zp, zc, q, k, v = outs[:5]
            if cache is None:
                states.append(outs[5:])
                att = _attn_call(q, k, v, None, None, *tiles["attn"])
            else:
                kc, vc = _cachekv_call(cache[0], cache[1], l, lw["w_ukv"], place)
                att = _attn_call(q, k, v, kc, vc, *tiles["attn"])
            x, h2 = _mix_call(x, zp, zc, att, mod, per_batch_mod, lw, tiles["mix"])
            x = _ffn_call(x, h2, mod, per_batch_mod, lw, fg, l == DEPTH - 1, *tiles["ffn"])
        return x, states

    y_prompt, states = run(x_prompt, lambda l: mods[l, n_dec:n_dec + 1], False, None, None, CTX_TILES)
    y_sample, _ = run(x_sample, lambda l: mods[l, :n_dec], True, rope_tab, (cache_ckv, cache_krope), DEC_TILES)
    state_ckv = jnp.stack([s[0] for s in states], axis=1)
    state_krope = jnp.stack([s[1] for s in states], axis=1)
    return (y_prompt, y_sample, state_ckv, state_krope)
```

```python
import functools
import math

import numpy as np
import jax
import jax.numpy as jnp
from jax import lax
from jax.experimental import pallas as pl
from jax.experimental.pallas import tpu as pltpu

D_MODEL = 1024
DEPTH = 4
GRID_W = 64
POOL_W = 256
POOL_GROUPS = 4
POOL_GW = 64
POOL_WINDOWS = (2, 4, 8, 16)
CONV_W = 256
CONV_K = 31
MLA_W = 512
N_HEADS = 4
V_DIM = 128
NOPE_DIM = 128
ROPE_DIM = 64
Q_LORA = 384
KV_LORA = 256
IN_W = POOL_W + Q_LORA + KV_LORA + ROPE_DIM + 2 * CONV_W
D_FF = 2816
FFN_K = 3
ROPE_THETA = 10000.0
EPS = 1e-6

LANES = 128
SUBLANES = 8
HEAD_W = 2 * LANES
IN_W_PAD = IN_W + ROPE_DIM
HALO = 16
FF_CHUNK = 256
KEY_CHUNK = 512
N_FF_CHUNKS = D_FF // FF_CHUNK
VMEM_LIMIT = 56 * 1024 * 1024

CTX_TILES = {"inproj": 256, "attn": (256, 256, 4), "mix": 256, "ffn": (2, 256)}
DEC_TILES = {"inproj": 512, "attn": (2048, 512, 1), "mix": 512, "ffn": (1, 512)}

F32 = jnp.float32
BF16 = jnp.bfloat16


def _bf(x):
    return x.astype(BF16)


def _dot(a, b):
    return jnp.dot(a, b, preferred_element_type=F32)


def _rms(x, g):
    return x * lax.rsqrt(jnp.mean(x * x, axis=-1, keepdims=True) + EPS) * g


def _silu(x):
    return x * jax.nn.sigmoid(x)


def _params(n_parallel):
    return pltpu.CompilerParams(dimension_semantics=("parallel",) * n_parallel,
                                vmem_limit_bytes=VMEM_LIMIT)


def _mod_kernel(c_ref, w_ref, b_ref, o_ref):
    o_ref[0] = _dot(_bf(_silu(c_ref[...])), _bf(w_ref[0])) + b_ref[0]


def _mod_call(c_all, ada_w, ada_b):
    R = c_all.shape[0]
    tn = 1536
    return pl.pallas_call(
        _mod_kernel,
        grid=(DEPTH, 6 * D_MODEL // tn),
        in_specs=[pl.BlockSpec((R, D_MODEL), lambda l, j: (0, 0)),
                  pl.BlockSpec((1, D_MODEL, tn), lambda l, j: (l, 0, j)),
                  pl.BlockSpec((1, 1, tn), lambda l, j: (l, 0, j))],
        out_specs=pl.BlockSpec((1, R, tn), lambda l, j: (l, 0, j)),
        out_shape=jax.ShapeDtypeStruct((DEPTH, R, 6 * D_MODEL), F32),
        compiler_params=_params(2),
        name="adaln_mod",
    )(c_all, ada_w, ada_b)


def _inproj_kernel(rope, state, x_ref, mod_ref, g1_ref, win_ref, qg_ref, wuq_ref, kvg_ref, wukv_ref,
                   *rest):
    if rope:
        rope_ref, rest = rest[0], rest[1:]
    zp_ref, zc_ref, q_ref, k_ref, v_ref = rest[:5]
    x = x_ref[0]
    mod = mod_ref[0]
    h = _rms(x, g1_ref[...]) * (1.0 + mod[1:2]) + mod[0:1]
    z = _dot(_bf(h), win_ref[...])
    o1 = POOL_W
    o2 = o1 + Q_LORA
    o3 = o2 + KV_LORA
    o4 = o3 + 2 * CONV_W
    zp_ref[0] = z[:, :o1]
    zc_ref[0] = z[:, o3:o3 + CONV_W] * jax.nn.sigmoid(z[:, o3 + CONV_W:o4])
    kr = z[:, o4:]
    scale = math.log2(math.e) / math.sqrt(NOPE_DIM + ROPE_DIM)
    qn = _rms(z[:, o1:o2], qg_ref[...] * scale)
    q = _dot(_bf(qn), wuq_ref[...])
    ckv = _rms(z[:, o2:o3], kvg_ref[...])
    kv = _dot(_bf(ckv), wukv_ref[...])
    lane = lax.broadcasted_iota(jnp.int32, kr.shape, 1)
    if rope:
        tab = rope_ref[...]
        qa, qb, ka = tab[:, :LANES], tab[:, LANES:2 * LANES], tab[:, 2 * LANES:]
        krot = kr * ka + pltpu.roll(kr, ROPE_DIM, 1) * qb
    else:
        krot = jnp.where(lane < ROPE_DIM, kr, 0.0)
    for hd in range(N_HEADS):
        c0 = hd * HEAD_W
        q0 = q[:, c0:c0 + LANES]
        q1 = q[:, c0 + LANES:c0 + HEAD_W]
        if rope:
            q0 = q0 * qa + q1 * qb
        q_ref[0, :, c0:c0 + LANES] = _bf(q0)
        q_ref[0, :, c0 + LANES:c0 + HEAD_W] = _bf(q1)
        k_ref[0, :, c0:c0 + LANES] = _bf(kv[:, c0:c0 + LANES] + krot)
        k_ref[0, :, c0 + LANES:c0 + HEAD_W] = _bf(kv[:, c0 + LANES:c0 + HEAD_W])
    v_ref[0] = _bf(kv[:, N_HEADS * HEAD_W:].T)
    if state:
        ckv_ref, kr_ref = rest[5:7]
        ckv_ref[0] = ckv
        kr_ref[0] = kr[:, :ROPE_DIM]


def _inproj_call(x, mod, per_batch_mod, lw, rope_tab, state, tm):
    B, L, D = x.shape
    rope = rope_tab is not None
    full = lambda a: pl.BlockSpec(a.shape, lambda b, i: (0,) * a.ndim)
    tile = lambda w: pl.BlockSpec((1, tm, w), lambda b, i: (b, i, 0))
    mod_map = (lambda b, i: (b, 0, 0)) if per_batch_mod else (lambda b, i: (0, 0, 0))
    weights = [lw["g1"], lw["w_in"], lw["qg"], lw["w_uq"], lw["kvg"], lw["w_ukv"]]
    in_specs = [tile(D), pl.BlockSpec((1, 8, D), mod_map)] + [full(w) for w in weights]
    args = [x, mod] + weights
    if rope:
        in_specs.append(pl.BlockSpec((tm, 3 * LANES), lambda b, i: (i, 0)))
        args.append(rope_tab)
    widths = [(POOL_W, F32), (CONV_W, F32), (N_HEADS * HEAD_W, BF16), (N_HEADS * HEAD_W, BF16)]
    states = [(KV_LORA, F32), (ROPE_DIM, F32)] if state else []
    vt_spec = pl.BlockSpec((1, N_HEADS * V_DIM, tm), lambda b, i: (b, 0, i))
    vt_shape = jax.ShapeDtypeStruct((B, N_HEADS * V_DIM, L), BF16)
    return pl.pallas_call(
        functools.partial(_inproj_kernel, rope, state),
        grid=(B, L // tm),
        in_specs=in_specs,
        out_specs=[tile(w) for w, _ in widths] + [vt_spec] + [tile(w) for w, _ in states],
        out_shape=[jax.ShapeDtypeStruct((B, L, w), dt) for w, dt in widths] + [vt_shape]
        + [jax.ShapeDtypeStruct((B, L, w), dt) for w, dt in states],
        compiler_params=_params(2),
        name="inproj",
    )(*args)


def _cachekv_kernel(ckv_ref, kr_ref, wukv_ref, place_ref, k_ref, v_ref):
    kv = _dot(_bf(ckv_ref[0, 0]), wukv_ref[...])
    krot = _dot(_bf(kr_ref[0, 0]), place_ref[...])
    for hd in range(N_HEADS):
        c0 = hd * HEAD_W
        k_ref[0, :, c0:c0 + LANES] = _bf(kv[:, c0:c0 + LANES] + krot)
        k_ref[0, :, c0 + LANES:c0 + HEAD_W] = _bf(kv[:, c0 + LANES:c0 + HEAD_W])
    v_ref[0] = _bf(kv[:, N_HEADS * HEAD_W:].T)


def _cachekv_call(cache_ckv, cache_krope, layer, w_ukv, place):
    B, _, Lc, _ = cache_ckv.shape
    return pl.pallas_call(
        _cachekv_kernel,
        grid=(B,),
        in_specs=[pl.BlockSpec((1, 1, Lc, KV_LORA), lambda b: (b, layer, 0, 0)),
                  pl.BlockSpec((1, 1, Lc, ROPE_DIM), lambda b: (b, layer, 0, 0)),
                  pl.BlockSpec(w_ukv.shape, lambda b: (0, 0)),
                  pl.BlockSpec(place.shape, lambda b: (0, 0))],
        out_specs=[pl.BlockSpec((1, Lc, N_HEADS * HEAD_W), lambda b: (b, 0, 0)),
                   pl.BlockSpec((1, N_HEADS * V_DIM, Lc), lambda b: (b, 0, 0))],
        out_shape=[jax.ShapeDtypeStruct((B, Lc, N_HEADS * HEAD_W), BF16),
                   jax.ShapeDtypeStruct((B, N_HEADS * V_DIM, Lc), BF16)],
        compiler_params=_params(1),
        name="cache_kv",
    )(cache_ckv, cache_krope, w_ukv, place)


def _nt_dot(a, b):
    return lax.dot_general(a, b, (((1,), (1,)), ((), ())), preferred_element_type=F32)


def _attn_kernel(ctx, sub, heads, q_ref, k_ref, v_ref, *rest):
    if ctx:
        kc_ref, vc_ref, o_ref, s_buf = rest
    else:
        o_ref, s_buf = rest
    lk = k_ref.shape[1]
    ch = min(KEY_CHUNK, lk)
    chunks = [(k_ref, v_ref, r, r) for r in range(0, lk, ch)]
    if ctx:
        chunks += [(kc_ref, vc_ref, r, lk + r) for r in range(0, kc_ref.shape[1], ch)]
    items = [(hd, t) for hd in range(heads) for t in range(q_ref.shape[1] // sub)]

    def scores(i, c):
        hd, t = items[i]
        keys, _, r, row = chunks[c]
        s_buf[i % 2, row:row + ch, :] = _nt_dot(keys[0, r:r + ch, hd * HEAD_W:(hd + 1) * HEAD_W],
                                                q_ref[0, t * sub:(t + 1) * sub, hd * HEAD_W:(hd + 1) * HEAD_W])

    for c in range(len(chunks)):
        scores(0, c)
    for i, (hd, t) in enumerate(items):
        m = jnp.max(s_buf[i % 2], axis=0, keepdims=True)
        l = jnp.zeros((1, sub), F32)
        o = jnp.zeros((V_DIM, sub), F32)
        for c, (_, values, r, row) in enumerate(chunks):
            if i + 1 < len(items):
                scores(i + 1, c)
            e = jnp.exp2(s_buf[i % 2, row:row + ch, :] - m)
            l = l + jnp.sum(e, axis=0, keepdims=True)
            o = o + _dot(values[0, hd * V_DIM:(hd + 1) * V_DIM, r:r + ch], _bf(e))
        o_ref[0, t * sub:(t + 1) * sub, hd * V_DIM:(hd + 1) * V_DIM] = _bf((o / l).T)


def _attn_call(q, k, v, kc, vc, tq, sub, heads):
    B, L, _ = q.shape
    ctx = kc is not None
    k_spec = lambda a: pl.BlockSpec((1, a.shape[1], heads * HEAD_W), lambda b, h, i: (b, 0, h))
    v_spec = lambda a: pl.BlockSpec((1, heads * V_DIM, a.shape[2]), lambda b, h, i: (b, h, 0))
    in_specs = [pl.BlockSpec((1, tq, heads * HEAD_W), lambda b, h, i: (b, i, h)), k_spec(k), v_spec(v)]
    args = [q, k, v]
    if ctx:
        in_specs += [k_spec(kc), v_spec(vc)]
        args += [kc, vc]
    return pl.pallas_call(
        functools.partial(_attn_kernel, ctx, sub, heads),
        grid=(B, N_HEADS // heads, L // tq),
        in_specs=in_specs,
        out_specs=pl.BlockSpec((1, tq, heads * V_DIM), lambda b, h, i: (b, i, h)),
        out_shape=jax.ShapeDtypeStruct((B, L, N_HEADS * V_DIM), BF16),
        scratch_shapes=[pltpu.VMEM((2, k.shape[1] + (kc.shape[1] if ctx else 0), sub), F32)],
        compiler_params=_params(3),
        name="attention",
    )(*args)


def _halo_specs(tm, width, seq_len):
    per, last = tm // HALO, seq_len // HALO - 1
    return [pl.BlockSpec((1, tm, width), lambda b, i: (b, i, 0)),
            pl.BlockSpec((1, HALO, width), lambda b, i: (b, jnp.maximum(i * per - 1, 0), 0)),
            pl.BlockSpec((1, HALO, width), lambda b, i: (b, jnp.minimum((i + 1) * per, last), 0))]


def _halo_fill(dst_ref, tile_ref, left_ref, right_ref, i, n_tiles):
    tm = tile_ref.shape[1]
    zero = jnp.zeros((HALO, dst_ref.shape[1]), dst_ref.dtype)
    dst_ref[HALO:HALO + tm, :] = tile_ref[0]
    dst_ref[0:HALO, :] = jnp.where(i > 0, left_ref[0], zero)
    dst_ref[HALO + tm:, :] = jnp.where(i < n_tiles - 1, right_ref[0], zero)


def _mix_kernel(tm, seq_len, x_ref, zp_ref, zp_l, zp_r, zc_ref, zc_l, zc_r, att_ref, mod_ref, pw_ref, ps_ref,
                cw_ref, cb_ref, lng_ref, lnb_ref, wout_ref, g2_ref, xo_ref, h2_ref, pbuf, cbuf, cstage, wbuf,
                ybuf, ymix):
    i = pl.program_id(1)
    n_tiles = seq_len // tm
    _halo_fill(pbuf, zp_ref, zp_l, zp_r, i, n_tiles)
    _halo_fill(cbuf, zc_ref, zc_l, zc_r, i, n_tiles)

    z = pbuf[...]
    n = tm + 2 * HALO
    t = i * tm + lax.broadcasted_iota(jnp.int32, (tm, LANES), 0)
    lane = lax.broadcasted_iota(jnp.int32, (tm, LANES), 1)

    def count(win):
        return (jnp.minimum(t + win // 2, seq_len) - jnp.maximum(t - win // 2, 0)).astype(F32)

    zl, zr = z[:, :LANES], z[:, LANES:]
    a2l = zl[0:n - 1] + zl[1:n]
    a4l = a2l[0:n - 3] + a2l[2:n - 1]
    a2r = zr[0:n - 1] + zr[1:n]
    a4r = a2r[0:n - 3] + a2r[2:n - 1]
    a8r = a4r[0:n - 7] + a4r[4:n - 3]
    a16r = a8r[0:n - 15] + a8r[8:n - 7]
    low = lane < POOL_GW
    pooled_l = jnp.where(low, a2l[HALO - 1:HALO - 1 + tm] / count(2), a4l[HALO - 2:HALO - 2 + tm] / count(4))
    pooled_r = jnp.where(low, a8r[HALO - 4:HALO - 4 + tm] / count(8), a16r[HALO - 8:HALO - 8 + tm] / count(16))
    zc_l, zc_r = zl[HALO:HALO + tm], zr[HALO:HALO + tm]
    d = jnp.concatenate([pooled_l - zc_l, pooled_r - zc_r], axis=1)
    o1, o2 = POOL_W, POOL_W + MLA_W
    ymix[:, 0:o1] = _bf(_dot(_bf(d), pw_ref[...]) * ps_ref[...])
    ymix[:, o1:o2] = att_ref[0]

    seg = tm // SUBLANES
    pitch = seg + 2 * HALO + 4
    n_slabs = CONV_W // LANES
    for s in range(SUBLANES):
        for k in range(n_slabs):
            cstage[k, s * pitch:s * pitch + seg + 2 * HALO, :] = (
                cbuf[s * seg:s * seg + seg + 2 * HALO, k * LANES:(k + 1) * LANES])
    for kk in range(CONV_K):
        wbuf[kk] = jnp.broadcast_to(cw_ref[kk:kk + 1, :], (SUBLANES, CONV_W))
    jb = 16
    first = HALO - CONV_K // 2
    cols = []
    for k in range(n_slabs):
        bias = jnp.broadcast_to(cb_ref[:, k * LANES:(k + 1) * LANES], (SUBLANES, LANES))
        groups = []
        for j0 in range(0, seg, jb):
            accs = [bias] * jb
            for p in range(j0, j0 + jb + CONV_K - 1):
                xrow = cstage[k, pl.ds(first + p, SUBLANES, stride=pitch), :]
                for j in range(max(j0, p - CONV_K + 1), min(j0 + jb - 1, p) + 1):
                    accs[j - j0] = accs[j - j0] + xrow * wbuf[p - j, :, k * LANES:(k + 1) * LANES]
            groups += accs
        cols.append(jnp.concatenate(groups, axis=0))
    u = jnp.concatenate(cols, axis=1)
    mu = jnp.mean(u, axis=-1, keepdims=True)
    var = jnp.mean(jnp.square(u - mu), axis=-1, keepdims=True)
    y_perm = _silu((u - mu) * lax.rsqrt(var + EPS) * lng_ref[...] + lnb_ref[...])
    for k in range(n_slabs):
        ybuf[k] = y_perm[:, k * LANES:(k + 1) * LANES]
    y_conv = jnp.concatenate(
        [jnp.concatenate([ybuf[k, pl.ds(s + SUBLANES * SUBLANES * m, SUBLANES, stride=SUBLANES), :]
                          for s in range(SUBLANES) for m in range(seg // SUBLANES)], axis=0)
         for k in range(n_slabs)], axis=1)
    ymix[:, o2:] = _bf(y_conv)

    mod = mod_ref[0]
    x = x_ref[0] + mod[2:3] * _dot(ymix[...], wout_ref[...])
    xo_ref[0] = x
    h2_ref[0] = _bf(_rms(x, g2_ref[...]) * (1.0 + mod[4:5]) + mod[3:4])


def _once(a):
    return pl.BlockSpec(a.shape, lambda b, i: (0,) * a.ndim, pipeline_mode=pl.Buffered(1))


def _mix_plan(x, zp, zc, att, mod, per_batch_mod, lw, tm):
    B, L, D = x.shape
    tile = lambda w: pl.BlockSpec((1, tm, w), lambda b, i: (b, i, 0))
    mod_map = (lambda b, i: (b, 0, 0)) if per_batch_mod else (lambda b, i: (0, 0, 0))
    weights = [lw["pool_bd"], lw["pool_scale"], lw["conv_w"], lw["conv_b"], lw["ln_g"], lw["ln_b"],
               lw["w_out"], lw["g2"]]
    return dict(
        kernel=functools.partial(_mix_kernel, tm, L),
        grid=(B, L // tm),
        in_specs=[tile(D)] + _halo_specs(tm, POOL_W, L) + _halo_specs(tm, CONV_W, L)
        + [tile(MLA_W), pl.BlockSpec((1, 8, D), mod_map)] + [_once(w) for w in weights],
        out_specs=[tile(D), tile(D)],
        out_shape=[jax.ShapeDtypeStruct((B, L, D), F32), jax.ShapeDtypeStruct((B, L, D), BF16)],
        scratch_shapes=[pltpu.VMEM((tm + 2 * HALO, POOL_W), F32), pltpu.VMEM((tm + 2 * HALO, CONV_W), F32),
                        pltpu.VMEM((CONV_W // LANES, SUBLANES * (tm // SUBLANES + 2 * HALO + 4), LANES), F32),
                        pltpu.VMEM((CONV_K + 1, SUBLANES, CONV_W), F32),
                        pltpu.VMEM((CONV_W // LANES, tm, LANES), F32),
                        pltpu.VMEM((tm, D), BF16)],
        args=[x, zp, zp, zp, zc, zc, zc, att, mod] + weights,
        name="mixer")


def _launch(plan):
    return pl.pallas_call(
        plan["kernel"], grid=plan["grid"], in_specs=plan["in_specs"], out_specs=plan["out_specs"],
        out_shape=plan["out_shape"], scratch_shapes=plan["scratch_shapes"],
        compiler_params=_params(len(plan["grid"])), name=plan["name"])(*plan["args"])


def _ffn_kernel(nq, rows, final, x_ref, h2_ref, h2_l, h2_r, mod_ref, wup_ref, cw_ref, cb_ref, wdn_ref, fg_ref,
                o_ref, hbuf, stage, pbuf, abuf, ybuf):
    tm = nq * rows
    seg = tm // SUBLANES
    per_piece = SUBLANES // nq
    pitch = seg + 2 * HALO + 4
    n_slabs = D_MODEL // LANES

    if nq == 1:
        _halo_fill(hbuf, h2_ref, h2_l, h2_r, pl.program_id(1), pl.num_programs(1))
    else:
        for q in range(nq + 1):
            hbuf[q * (rows + HALO):q * (rows + HALO) + HALO, :] = jnp.zeros((HALO, D_MODEL), BF16)
        for q in range(nq):
            hbuf[q * (rows + HALO) + HALO:(q + 1) * (rows + HALO), :] = h2_ref[q]

    for s in range(SUBLANES):
        src = (s // per_piece) * (rows + HALO) + (s % per_piece) * seg
        for k in range(n_slabs):
            stage[k, s * pitch:s * pitch + seg + 2 * HALO, :] = (
                hbuf[src:src + seg + 2 * HALO, k * LANES:(k + 1) * LANES].astype(F32))

    for g in range(seg // 2 + 1):
        for k in range(n_slabs):
            lo = stage[k, pl.ds(HALO - 1 + 2 * g, SUBLANES, stride=pitch), :]
            hi = stage[k, pl.ds(HALO + 2 * g, SUBLANES, stride=pitch), :]
            pbuf[g * 2 * SUBLANES:(g + 1) * 2 * SUBLANES, k * LANES:(k + 1) * LANES] = (
                _bf(jnp.concatenate([lo, hi], axis=0)))

    def conv_up(c0):
        u = _dot(pbuf[...], wup_ref[:, c0:c0 + FF_CHUNK])
        cw = cw_ref[:, c0:c0 + FF_CHUNK]
        return (u[0:tm] * cw[0:1] + u[SUBLANES:tm + SUBLANES] * cw[1:2]
                + u[2 * SUBLANES:tm + 2 * SUBLANES] * cw[2:3] + cb_ref[:, c0:c0 + FF_CHUNK])

    for j in range(N_FF_CHUNKS):
        gate_j, value_j = conv_up(j * FF_CHUNK), conv_up(D_FF + j * FF_CHUNK)
        abuf[:, j * FF_CHUNK:(j + 1) * FF_CHUNK] = _bf(_silu(gate_j) * value_j)

    y = _dot(abuf[...], wdn_ref[...])
    for k in range(n_slabs):
        ybuf[k] = y[:, k * LANES:(k + 1) * LANES]
    gate = mod_ref[0][5:6]
    for s in range(SUBLANES):
        q, r0 = s // per_piece, (s % per_piece) * seg
        ys = jnp.concatenate(
            [jnp.concatenate([ybuf[k, pl.ds(s + SUBLANES * SUBLANES * m, SUBLANES, stride=SUBLANES), :]
                              for m in range(seg // SUBLANES)], axis=0) for k in range(n_slabs)], axis=1)
        x = x_ref[q, r0:r0 + seg, :] + gate * ys
        o_ref[q, r0:r0 + seg, :] = _rms(x, fg_ref[...]) if final else x


def _ffn_plan(x, h2, mod, per_batch_mod, lw, final_g, final, nq, rows):
    B, L, D = x.shape
    assert (nq == 1 and L % rows == 0) or (rows == L and B % nq == 0)
    tm = nq * rows
    seg = tm // SUBLANES
    tile = pl.BlockSpec((nq, rows, D), lambda b, i: (b, i, 0))
    halos = _halo_specs(rows, D, L)[1:] if nq == 1 else [pl.BlockSpec((1, HALO, D), lambda b, i: (b, 0, 0))] * 2
    mod_map = (lambda b, i: (b, 0, 0)) if per_batch_mod else (lambda b, i: (0, 0, 0))
    weights = [lw["w_up"], lw["ffn_cw"], lw["ffn_cb"], lw["w_down"], final_g]
    return dict(
        kernel=functools.partial(_ffn_kernel, nq, rows, final),
        grid=(B // nq, L // rows),
        in_specs=[tile, tile] + halos + [pl.BlockSpec((1, 8, D), mod_map)] + [_once(w) for w in weights],
        out_specs=[tile],
        out_shape=[jax.ShapeDtypeStruct((B, L, D), F32)],
        scratch_shapes=[pltpu.VMEM((nq * (rows + HALO) + HALO, D), BF16),
                        pltpu.VMEM((D // LANES, SUBLANES * (seg + 2 * HALO + 4), LANES), F32),
                        pltpu.VMEM(((seg + 2) * SUBLANES, D), BF16),
                        pltpu.VMEM((tm, D_FF), BF16),
                        pltpu.VMEM((D // LANES, tm, LANES), F32)],
        args=[x, h2, h2, h2, mod] + weights,
        name="conv_ffn")


def _layout_indices():
    sw = np.arange(ROPE_DIM) ^ (ROPE_DIM // 4)
    o3 = POOL_W + Q_LORA + KV_LORA
    o4 = o3 + ROPE_DIM
    idx_in = np.concatenate([np.arange(0, o3), np.arange(o4, IN_W), np.arange(o3, o4), o3 + sw])
    qh = NOPE_DIM + ROPE_DIM
    half = NOPE_DIM // 2
    idx_uq = np.concatenate([
        np.concatenate([h * qh + NOPE_DIM + np.arange(ROPE_DIM), h * qh + np.arange(half),
                        h * qh + NOPE_DIM + sw, h * qh + half + np.arange(half)])
        for h in range(N_HEADS)])
    kvh = NOPE_DIM + V_DIM
    zero_col = N_HEADS * kvh
    zeros = np.full(half, zero_col)
    idx_k = np.concatenate([
        np.concatenate([zeros, h * kvh + np.arange(half), zeros, h * kvh + half + np.arange(half)])
        for h in range(N_HEADS)])
    idx_v = np.concatenate([h * kvh + NOPE_DIM + np.arange(V_DIM) for h in range(N_HEADS)])
    return idx_in, idx_uq, np.concatenate([idx_k, idx_v])


def _take_cols(w, idx, zero_col):
    parts, start = [], 0
    for i in range(1, len(idx) + 1):
        run_ends = i == len(idx) or (idx[i] != idx[i - 1] + 1 and not (idx[i] == idx[i - 1] == zero_col))
        if run_ends:
            a, n = int(idx[start]), i - start
            parts.append(jnp.zeros((w.shape[0], n), BF16) if a == zero_col else _bf(w[:, a:a + n]))
            start = i
    return jnp.concatenate(parts, axis=1)


def _prep_layer(l, p):
    idx_in, idx_uq, idx_ukv = _layout_indices()
    row = lambda a: a[l][None, :]
    eye = jnp.eye(POOL_GROUPS, dtype=F32)
    pool_bd = (eye[:, None, :, None] * p["pool_w"][l][:, :, None, :]).reshape(POOL_W, POOL_W)
    return {
        "g1": row(p["norm1_g"]),
        "w_in": _take_cols(p["w_in"][l], idx_in, -1),
        "qg": row(p["q_norm_g"]),
        "w_uq": _take_cols(p["w_uq"][l], idx_uq, -1),
        "kvg": row(p["kv_norm_g"]),
        "w_ukv": _take_cols(p["w_ukv"][l], idx_ukv, N_HEADS * (NOPE_DIM + V_DIM)),
        "pool_bd": _bf(pool_bd),
        "pool_scale": row(p["pool_scale"]),
        "conv_w": jnp.pad(p["conv_w"][l], ((0, 1), (0, 0))),
        "conv_b": row(p["conv_b"]),
        "ln_g": row(p["conv_ln_g"]),
        "ln_b": row(p["conv_ln_b"]),
        "w_out": _bf(p["w_out"][l]),
        "g2": row(p["norm2_g"]),
        "w_up": _bf(p["w_up"][l]),
        "ffn_cw": jnp.pad(p["ffn_conv_w"][l], ((0, SUBLANES - FFN_K), (0, 0))),
        "ffn_cb": row(p["ffn_conv_b"]),
        "w_down": _bf(p["w_down"][l]),
    }


def _rope_table(L):
    rows = L // GRID_W
    row = np.repeat(np.arange(rows, dtype=np.float32), GRID_W)
    col = np.tile(np.arange(GRID_W, dtype=np.float32), rows)
    n = ROPE_DIM // 4
    inv = np.power(np.float32(ROPE_THETA), -np.arange(n, dtype=np.float32) / np.float32(n)).astype(np.float32)
    ang_r, ang_c = row[:, None] * inv, col[:, None] * inv
    cos = np.concatenate([np.cos(ang_r)] * 2 + [np.cos(ang_c)] * 2, axis=1)
    sin = np.concatenate([-np.sin(ang_r), np.sin(ang_r), -np.sin(ang_c), np.sin(ang_c)], axis=1)
    one, zero = np.ones_like(cos), np.zeros_like(cos)
    table = np.concatenate([cos, one, sin, zero, cos, zero], axis=1)
    return jnp.asarray(table, dtype=F32)


def kernel(x_prompt, x_sample, cache_ckv, cache_krope, c, c_ctx, ada_w, ada_b, norm1_g, w_in, pool_w,
           pool_scale, q_norm_g, w_uq, kv_norm_g, w_ukv, conv_w, conv_b, conv_ln_g, conv_ln_b, w_out,
           norm2_g, w_up, ffn_conv_w, ffn_conv_b, w_down, final_g):
    p = dict(norm1_g=norm1_g, w_in=w_in, pool_w=pool_w, pool_scale=pool_scale, q_norm_g=q_norm_g,
             w_uq=w_uq, kv_norm_g=kv_norm_g, w_ukv=w_ukv, conv_w=conv_w, conv_b=conv_b,
             conv_ln_g=conv_ln_g, conv_ln_b=conv_ln_b, w_out=w_out, norm2_g=norm2_g, w_up=w_up,
             ffn_conv_w=ffn_conv_w, ffn_conv_b=ffn_conv_b, w_down=w_down)
    n_dec = c.shape[0]
    c_all = jnp.concatenate([c, c_ctx[None, :], jnp.zeros((16 - n_dec - 1, D_MODEL), F32)], axis=0)
    mods = _mod_call(c_all, ada_w, ada_b.reshape(DEPTH, 1, 6 * D_MODEL))
    mods = jnp.pad(mods.reshape(DEPTH, 16, 6, D_MODEL), ((0, 0), (0, 0), (0, 2), (0, 0)))
    layers = [_prep_layer(l, p) for l in range(DEPTH)]
    fg = final_g[None, :]
    place = _bf(jnp.eye(ROPE_DIM, LANES, dtype=F32))
    rope_tab = _rope_table(x_sample.shape[1])

    def run(x, mod_of, per_batch_mod, rope, cache, tiles):
        states = []
        for l in range(DEPTH):
            lw, mod = layers[l], mod_of(l)
            outs = _inproj_call(x, mod, per_batch_mod, lw, rope, cache is None, tiles["inproj"])
            zp, zc, q, k, v = outs[:5]
            if cache is None:
                states.append(outs[5:])
                att = _attn_call(q, k, v, None, None, *tiles["attn"])
            else:
                kc, vc = _cachekv_call(cache[0], cache[1], l, lw["w_ukv"], place)
                att = _attn_call(q, k, v, kc, vc, *tiles["attn"])
            x, h2 = _launch(_mix_plan(x, zp, zc, att, mod, per_batch_mod, lw, tiles["mix"]))
            x, = _launch(_ffn_plan(x, h2, mod, per_batch_mod, lw, fg, l == DEPTH - 1, *tiles["ffn"]))
        return x, states

    y_prompt, states = run(x_prompt, lambda l: mods[l, n_dec:n_dec + 1], False, None, None, CTX_TILES)
    y_sample, _ = run(x_sample, lambda l: mods[l, :n_dec], True, rope_tab, (cache_ckv, cache_krope), DEC_TILES)
    state_ckv = jnp.stack([s[0] for s in states], axis=1)
    state_krope = jnp.stack([s[1] for s in states], axis=1)
    return (y_prompt, y_sample, state_ckv, state_krope)
```

```python
import functools
import math

import numpy as np
import jax
import jax.numpy as jnp
from jax import lax
from jax.experimental import pallas as pl
from jax.experimental.pallas import tpu as pltpu

D_MODEL = 1024
DEPTH = 4
GRID_W = 64
POOL_W = 256
POOL_GROUPS = 4
POOL_GW = 64
POOL_WINDOWS = (2, 4, 8, 16)
CONV_W = 256
CONV_K = 31
MLA_W = 512
N_HEADS = 4
V_DIM = 128
NOPE_DIM = 128
ROPE_DIM = 64
Q_LORA = 384
KV_LORA = 256
IN_W = POOL_W + Q_LORA + KV_LORA + ROPE_DIM + 2 * CONV_W
D_FF = 2816
FFN_K = 3
ROPE_THETA = 10000.0
EPS = 1e-6

LANES = 128
SUBLANES = 8
HEAD_W = 2 * LANES
IN_W_PAD = IN_W + ROPE_DIM
HALO = 16
FF_CHUNK = 256
KEY_CHUNK = 512
CACHE_BATCHES_PER_STEP = 4
N_FF_CHUNKS = D_FF // FF_CHUNK
VMEM_LIMIT = 56 * 1024 * 1024

CTX_TILES = {"inproj": 256, "attn": (256, 256, 4), "mix": 256, "ffn": (2, 256)}
DEC_TILES = {"inproj": 512, "attn": (2048, 512, 1), "mix": 512, "ffn": (1, 512)}

F32 = jnp.float32
BF16 = jnp.bfloat16


def _bf(x):
    return x.astype(BF16)


def _dot(a, b):
    return jnp.dot(a, b, preferred_element_type=F32)


def _rms(x, g):
    return x * lax.rsqrt(jnp.mean(x * x, axis=-1, keepdims=True) + EPS) * g


def _silu(x):
    return x * jax.nn.sigmoid(x)


def _params(n_parallel):
    return pltpu.CompilerParams(dimension_semantics=("parallel",) * n_parallel,
                                vmem_limit_bytes=VMEM_LIMIT)


def _mod_kernel(c_ref, w_ref, b_ref, o_ref):
    o_ref[0] = _dot(_bf(_silu(c_ref[...])), _bf(w_ref[0])) + b_ref[0]


def _mod_call(c_all, ada_w, ada_b):
    R = c_all.shape[0]
    tn = 1536
    return pl.pallas_call(
        _mod_kernel,
        grid=(DEPTH, 6 * D_MODEL // tn),
        in_specs=[pl.BlockSpec((R, D_MODEL), lambda l, j: (0, 0)),
                  pl.BlockSpec((1, D_MODEL, tn), lambda l, j: (l, 0, j)),
                  pl.BlockSpec((1, 1, tn), lambda l, j: (l, 0, j))],
        out_specs=pl.BlockSpec((1, R, tn), lambda l, j: (l, 0, j)),
        out_shape=jax.ShapeDtypeStruct((DEPTH, R, 6 * D_MODEL), F32),
        compiler_params=_params(2),
        name="adaln_mod",
    )(c_all, ada_w, ada_b)


def _inproj_kernel(rope, state, x_ref, mod_ref, g1_ref, win_ref, qg_ref, wuq_ref, kvg_ref, wukv_ref,
                   *rest):
    if rope:
        rope_ref, rest = rest[0], rest[1:]
    zp_ref, zc_ref, q_ref, k_ref, v_ref = rest[:5]
    x = x_ref[0]
    mod = mod_ref[0]
    h = _rms(x, g1_ref[...]) * (1.0 + mod[1:2]) + mod[0:1]
    z = _dot(_bf(h), win_ref[...])
    o1 = POOL_W
    o2 = o1 + Q_LORA
    o3 = o2 + KV_LORA
    o4 = o3 + 2 * CONV_W
    zp_ref[0] = z[:, :o1]
    zc_ref[0] = z[:, o3:o3 + CONV_W] * jax.nn.sigmoid(z[:, o3 + CONV_W:o4])
    kr = z[:, o4:]
    scale = math.log2(math.e) / math.sqrt(NOPE_DIM + ROPE_DIM)
    qn = _rms(z[:, o1:o2], qg_ref[...] * scale)
    q = _dot(_bf(qn), wuq_ref[...])
    ckv = _rms(z[:, o2:o3], kvg_ref[...])
    kv = _dot(_bf(ckv), wukv_ref[...])
    lane = lax.broadcasted_iota(jnp.int32, kr.shape, 1)
    if rope:
        tab = rope_ref[...]
        qa, qb, ka = tab[:, :LANES], tab[:, LANES:2 * LANES], tab[:, 2 * LANES:]
        krot = kr * ka + pltpu.roll(kr, ROPE_DIM, 1) * qb
    else:
        krot = jnp.where(lane < ROPE_DIM, kr, 0.0)
    for hd in range(N_HEADS):
        c0 = hd * HEAD_W
        q0 = q[:, c0:c0 + LANES]
        q1 = q[:, c0 + LANES:c0 + HEAD_W]
        if rope:
            q0 = q0 * qa + q1 * qb
        q_ref[0, :, c0:c0 + LANES] = _bf(q0)
        q_ref[0, :, c0 + LANES:c0 + HEAD_W] = _bf(q1)
        k_ref[0, :, c0:c0 + LANES] = _bf(kv[:, c0:c0 + LANES] + krot)
        k_ref[0, :, c0 + LANES:c0 + HEAD_W] = _bf(kv[:, c0 + LANES:c0 + HEAD_W])
    v_ref[0] = _bf(kv[:, N_HEADS * HEAD_W:].T)
    if state:
        ckv_ref, kr_ref = rest[5:7]
        ckv_ref[0] = ckv
        kr_ref[0] = kr[:, :ROPE_DIM]


def _inproj_call(x, mod, per_batch_mod, lw, rope_tab, state, tm):
    B, L, D = x.shape
    rope = rope_tab is not None
    full = lambda a: pl.BlockSpec(a.shape, lambda b, i: (0,) * a.ndim)
    tile = lambda w: pl.BlockSpec((1, tm, w), lambda b, i: (b, i, 0))
    mod_map = (lambda b, i: (b, 0, 0)) if per_batch_mod else (lambda b, i: (0, 0, 0))
    weights = [lw["g1"], lw["w_in"], lw["qg"], lw["w_uq"], lw["kvg"], lw["w_ukv"]]
    in_specs = [tile(D), pl.BlockSpec((1, 8, D), mod_map)] + [full(w) for w in weights]
    args = [x, mod] + weights
    if rope:
        in_specs.append(pl.BlockSpec((tm, 3 * LANES), lambda b, i: (i, 0)))
        args.append(rope_tab)
    widths = [(POOL_W, F32), (CONV_W, F32), (N_HEADS * HEAD_W, BF16), (N_HEADS * HEAD_W, BF16)]
    states = [(KV_LORA, F32), (ROPE_DIM, F32)] if state else []
    vt_spec = pl.BlockSpec((1, N_HEADS * V_DIM, tm), lambda b, i: (b, 0, i))
    vt_shape = jax.ShapeDtypeStruct((B, N_HEADS * V_DIM, L), BF16)
    return pl.pallas_call(
        functools.partial(_inproj_kernel, rope, state),
        grid=(B, L // tm),
        in_specs=in_specs,
        out_specs=[tile(w) for w, _ in widths] + [vt_spec] + [tile(w) for w, _ in states],
        out_shape=[jax.ShapeDtypeStruct((B, L, w), dt) for w, dt in widths] + [vt_shape]
        + [jax.ShapeDtypeStruct((B, L, w), dt) for w, dt in states],
        compiler_params=_params(2),
        name="inproj",
    )(*args)


def _cachekv_kernel(ckv_ref, kr_ref, wukv_ref, place_ref, k_ref, v_ref):
    for b in range(ckv_ref.shape[0]):
        kv = _dot(_bf(ckv_ref[b, 0]), wukv_ref[...])
        krot = _dot(_bf(kr_ref[b, 0]), place_ref[...])
        for hd in range(N_HEADS):
            c0 = hd * HEAD_W
            k_ref[b, :, c0:c0 + LANES] = _bf(kv[:, c0:c0 + LANES] + krot)
            k_ref[b, :, c0 + LANES:c0 + HEAD_W] = _bf(kv[:, c0 + LANES:c0 + HEAD_W])
        v_ref[b] = _bf(kv[:, N_HEADS * HEAD_W:].T)


def _cachekv_call(cache_ckv, cache_krope, layer, w_ukv, place):
    B, _, Lc, _ = cache_ckv.shape
    nb = CACHE_BATCHES_PER_STEP
    return pl.pallas_call(
        _cachekv_kernel,
        grid=(B // nb,),
        in_specs=[pl.BlockSpec((nb, 1, Lc, KV_LORA), lambda b: (b, layer, 0, 0)),
                  pl.BlockSpec((nb, 1, Lc, ROPE_DIM), lambda b: (b, layer, 0, 0)),
                  pl.BlockSpec(w_ukv.shape, lambda b: (0, 0)),
                  pl.BlockSpec(place.shape, lambda b: (0, 0))],
        out_specs=[pl.BlockSpec((nb, Lc, N_HEADS * HEAD_W), lambda b: (b, 0, 0)),
                   pl.BlockSpec((nb, N_HEADS * V_DIM, Lc), lambda b: (b, 0, 0))],
        out_shape=[jax.ShapeDtypeStruct((B, Lc, N_HEADS * HEAD_W), BF16),
                   jax.ShapeDtypeStruct((B, N_HEADS * V_DIM, Lc), BF16)],
        compiler_params=_params(1),
        name="cache_kv",
    )(cache_ckv, cache_krope, w_ukv, place)


def _nt_dot(a, b):
    return lax.dot_general(a, b, (((1,), (1,)), ((), ())), preferred_element_type=F32)


def _attn_kernel(ctx, sub, heads, q_ref, k_ref, v_ref, *rest):
    if ctx:
        kc_ref, vc_ref, o_ref, s_buf = rest
    else:
        o_ref, s_buf = rest
    lk = k_ref.shape[1]
    ch = min(KEY_CHUNK, lk)
    chunks = [(k_ref, v_ref, r, r) for r in range(0, lk, ch)]
    if ctx:
        chunks += [(kc_ref, vc_ref, r, lk + r) for r in range(0, kc_ref.shape[1], ch)]
    items = [(hd, t) for hd in range(heads) for t in range(q_ref.shape[1] // sub)]

    def scores(i, c):
        hd, t = items[i]
        keys, _, r, row = chunks[c]
        s_buf[i % 2, row:row + ch, :] = _nt_dot(keys[0, r:r + ch, hd * HEAD_W:(hd + 1) * HEAD_W],
                                                q_ref[0, t * sub:(t + 1) * sub, hd * HEAD_W:(hd + 1) * HEAD_W])

    for c in range(len(chunks)):
        scores(0, c)
    for i, (hd, t) in enumerate(items):
        m = jnp.max(s_buf[i % 2], axis=0, keepdims=True)
        l = jnp.zeros((1, sub), F32)
        o = jnp.zeros((V_DIM, sub), F32)
        for c, (_, values, r, row) in enumerate(chunks):
            if i + 1 < len(items):
                scores(i + 1, c)
            e = jnp.exp2(s_buf[i % 2, row:row + ch, :] - m)
            l = l + jnp.sum(e, axis=0, keepdims=True)
            o = o + _dot(values[0, hd * V_DIM:(hd + 1) * V_DIM, r:r + ch], _bf(e))
        o_ref[0, t * sub:(t + 1) * sub, hd * V_DIM:(hd + 1) * V_DIM] = _bf((o / l).T)


def _attn_call(q, k, v, kc, vc, tq, sub, heads):
    B, L, _ = q.shape
    ctx = kc is not None
    k_spec = lambda a: pl.BlockSpec((1, a.shape[1], heads * HEAD_W), lambda b, h, i: (b, 0, h))
    v_spec = lambda a: pl.BlockSpec((1, heads * V_DIM, a.shape[2]), lambda b, h, i: (b, h, 0))
    in_specs = [pl.BlockSpec((1, tq, heads * HEAD_W), lambda b, h, i: (b, i, h)), k_spec(k), v_spec(v)]
    args = [q, k, v]
    if ctx:
        in_specs += [k_spec(kc), v_spec(vc)]
        args += [kc, vc]
    return pl.pallas_call(
        functools.partial(_attn_kernel, ctx, sub, heads),
        grid=(B, N_HEADS // heads, L // tq),
        in_specs=in_specs,
        out_specs=pl.BlockSpec((1, tq, heads * V_DIM), lambda b, h, i: (b, i, h)),
        out_shape=jax.ShapeDtypeStruct((B, L, N_HEADS * V_DIM), BF16),
        scratch_shapes=[pltpu.VMEM((2, k.shape[1] + (kc.shape[1] if ctx else 0), sub), F32)],
        compiler_params=_params(3),
        name="attention",
    )(*args)


def _halo_specs(tm, width, seq_len):
    per, last = tm // HALO, seq_len // HALO - 1
    return [pl.BlockSpec((1, tm, width), lambda b, i: (b, i, 0)),
            pl.BlockSpec((1, HALO, width), lambda b, i: (b, jnp.maximum(i * per - 1, 0), 0)),
            pl.BlockSpec((1, HALO, width), lambda b, i: (b, jnp.minimum((i + 1) * per, last), 0))]


def _halo_fill(dst_ref, tile_ref, left_ref, right_ref, i, n_tiles):
    tm = tile_ref.shape[1]
    zero = jnp.zeros((HALO, dst_ref.shape[1]), dst_ref.dtype)
    dst_ref[HALO:HALO + tm, :] = tile_ref[0]
    dst_ref[0:HALO, :] = jnp.where(i > 0, left_ref[0], zero)
    dst_ref[HALO + tm:, :] = jnp.where(i < n_tiles - 1, right_ref[0], zero)


def _mix_kernel(tm, seq_len, x_ref, zp_ref, zp_l, zp_r, zc_ref, zc_l, zc_r, att_ref, mod_ref, rc_ref, pw_ref,
                ps_ref, cw_ref, cb_ref, lng_ref, lnb_ref, wout_ref, g2_ref, xo_ref, h2_ref, pbuf, cbuf, cstage,
                wbuf, ybuf, ymix):
    i = pl.program_id(1)
    n_tiles = seq_len // tm
    _halo_fill(pbuf, zp_ref, zp_l, zp_r, i, n_tiles)
    _halo_fill(cbuf, zc_ref, zc_l, zc_r, i, n_tiles)

    z = pbuf[...]
    n = tm + 2 * HALO
    lane = lax.broadcasted_iota(jnp.int32, (tm, LANES), 1)
    zl, zr = z[:, :LANES], z[:, LANES:]
    a2l = zl[0:n - 1] + zl[1:n]
    a4l = a2l[0:n - 3] + a2l[2:n - 1]
    a2r = zr[0:n - 1] + zr[1:n]
    a4r = a2r[0:n - 3] + a2r[2:n - 1]
    a8r = a4r[0:n - 7] + a4r[4:n - 3]
    a16r = a8r[0:n - 15] + a8r[8:n - 7]
    low = lane < POOL_GW
    inv_count = rc_ref[...]
    pooled_l = jnp.where(low, a2l[HALO - 1:HALO - 1 + tm], a4l[HALO - 2:HALO - 2 + tm]) * inv_count[:, :LANES]
    pooled_r = jnp.where(low, a8r[HALO - 4:HALO - 4 + tm], a16r[HALO - 8:HALO - 8 + tm]) * inv_count[:, LANES:]
    zc_l, zc_r = zl[HALO:HALO + tm], zr[HALO:HALO + tm]
    d = jnp.concatenate([pooled_l - zc_l, pooled_r - zc_r], axis=1)
    o1, o2 = POOL_W, POOL_W + MLA_W
    ymix[:, 0:o1] = _bf(_dot(_bf(d), pw_ref[...]) * ps_ref[...])
    ymix[:, o1:o2] = att_ref[0]

    seg = tm // SUBLANES
    pitch = seg + 2 * HALO + 4
    n_slabs = CONV_W // LANES
    for s in range(SUBLANES):
        for k in range(n_slabs):
            cstage[k, s * pitch:s * pitch + seg + 2 * HALO, :] = (
                cbuf[s * seg:s * seg + seg + 2 * HALO, k * LANES:(k + 1) * LANES])
    for kk in range(CONV_K):
        wbuf[kk] = jnp.broadcast_to(cw_ref[kk:kk + 1, :], (SUBLANES, CONV_W))
    jb = 16
    first = HALO - CONV_K // 2
    cols = []
    for k in range(n_slabs):
        bias = jnp.broadcast_to(cb_ref[:, k * LANES:(k + 1) * LANES], (SUBLANES, LANES))
        groups = []
        for j0 in range(0, seg, jb):
            accs = [bias] * jb
            for p in range(j0, j0 + jb + CONV_K - 1):
                xrow = cstage[k, pl.ds(first + p, SUBLANES, stride=pitch), :]
                for j in range(max(j0, p - CONV_K + 1), min(j0 + jb - 1, p) + 1):
                    accs[j - j0] = accs[j - j0] + xrow * wbuf[p - j, :, k * LANES:(k + 1) * LANES]
            groups += accs
        cols.append(jnp.concatenate(groups, axis=0))
    u = jnp.concatenate(cols, axis=1)
    mu = jnp.mean(u, axis=-1, keepdims=True)
    var = jnp.mean(jnp.square(u - mu), axis=-1, keepdims=True)
    y_perm = _silu((u - mu) * lax.rsqrt(var + EPS) * lng_ref[...] + lnb_ref[...])
    for k in range(n_slabs):
        ybuf[k] = y_perm[:, k * LANES:(k + 1) * LANES]
    y_conv = jnp.concatenate(
        [jnp.concatenate([ybuf[k, pl.ds(s + SUBLANES * SUBLANES * m, SUBLANES, stride=SUBLANES), :]
                          for s in range(SUBLANES) for m in range(seg // SUBLANES)], axis=0)
         for k in range(n_slabs)], axis=1)
    ymix[:, o2:] = _bf(y_conv)

    mod = mod_ref[0]
    x = x_ref[0] + mod[2:3] * _dot(ymix[...], wout_ref[...])
    xo_ref[0] = x
    h2_ref[0] = _bf(_rms(x, g2_ref[...] * (1.0 + mod[4:5])) + mod[3:4])


def _once(a):
    return pl.BlockSpec(a.shape, lambda b, i: (0,) * a.ndim, pipeline_mode=pl.Buffered(1))


def _pool_inv_count(L):
    t = np.arange(L)
    cols = [np.repeat((1.0 / (np.minimum(t + w // 2, L) - np.maximum(t - w // 2, 0)))[:, None], POOL_GW, axis=1)
            for w in POOL_WINDOWS]
    return jnp.asarray(np.concatenate(cols, axis=1), dtype=F32)


def _mix_plan(x, zp, zc, att, mod, per_batch_mod, lw, tm):
    B, L, D = x.shape
    tile = lambda w: pl.BlockSpec((1, tm, w), lambda b, i: (b, i, 0))
    mod_map = (lambda b, i: (b, 0, 0)) if per_batch_mod else (lambda b, i: (0, 0, 0))
    weights = [lw["pool_bd"], lw["pool_scale"], lw["conv_w"], lw["conv_b"], lw["ln_g"], lw["ln_b"],
               lw["w_out"], lw["g2"]]
    return dict(
        kernel=functools.partial(_mix_kernel, tm, L),
        grid=(B, L // tm),
        in_specs=[tile(D)] + _halo_specs(tm, POOL_W, L) + _halo_specs(tm, CONV_W, L)
        + [tile(MLA_W), pl.BlockSpec((1, 8, D), mod_map), pl.BlockSpec((tm, POOL_W), lambda b, i: (i, 0))]
        + [_once(w) for w in weights],
        out_specs=[tile(D), tile(D)],
        out_shape=[jax.ShapeDtypeStruct((B, L, D), F32), jax.ShapeDtypeStruct((B, L, D), BF16)],
        scratch_shapes=[pltpu.VMEM((tm + 2 * HALO, POOL_W), F32), pltpu.VMEM((tm + 2 * HALO, CONV_W), F32),
                        pltpu.VMEM((CONV_W // LANES, SUBLANES * (tm // SUBLANES + 2 * HALO + 4), LANES), F32),
                        pltpu.VMEM((CONV_K + 1, SUBLANES, CONV_W), F32),
                        pltpu.VMEM((CONV_W // LANES, tm, LANES), F32),
                        pltpu.VMEM((tm, D), BF16)],
        args=[x, zp, zp, zp, zc, zc, zc, att, mod, _pool_inv_count(L)] + weights,
        name="mixer")


def _launch(plan):
    return pl.pallas_call(
        plan["kernel"], grid=plan["grid"], in_specs=plan["in_specs"], out_specs=plan["out_specs"],
        out_shape=plan["out_shape"], scratch_shapes=plan["scratch_shapes"],
        compiler_params=_params(len(plan["grid"])), name=plan["name"])(*plan["args"])


def _ffn_kernel(nq, rows, final, x_ref, h2_ref, h2_l, h2_r, mod_ref, wup_ref, cw_ref, cb_ref, wdn_ref, fg_ref,
                o_ref, hbuf, stage, pbuf, abuf, ybuf):
    tm = nq * rows
    seg = tm // SUBLANES
    per_piece = SUBLANES // nq
    pitch = seg + 2 * HALO + 4
    n_slabs = D_MODEL // LANES

    if nq == 1:
        _halo_fill(hbuf, h2_ref, h2_l, h2_r, pl.program_id(1), pl.num_programs(1))
    else:
        for q in range(nq + 1):
            hbuf[q * (rows + HALO):q * (rows + HALO) + HALO, :] = jnp.zeros((HALO, D_MODEL), BF16)
        for q in range(nq):
            hbuf[q * (rows + HALO) + HALO:(q + 1) * (rows + HALO), :] = h2_ref[q]

    for s in range(SUBLANES):
        src = (s // per_piece) * (rows + HALO) + (s % per_piece) * seg
        for k in range(n_slabs):
            stage[k, s * pitch:s * pitch + seg + 2 * HALO, :] = (
                hbuf[src:src + seg + 2 * HALO, k * LANES:(k + 1) * LANES].astype(F32))

    for g in range(seg // 2 + 1):
        for k in range(n_slabs):
            lo = stage[k, pl.ds(HALO - 1 + 2 * g, SUBLANES, stride=pitch), :]
            hi = stage[k, pl.ds(HALO + 2 * g, SUBLANES, stride=pitch), :]
            pbuf[g * 2 * SUBLANES:(g + 1) * 2 * SUBLANES, k * LANES:(k + 1) * LANES] = (
                _bf(jnp.concatenate([lo, hi], axis=0)))

    def conv_up(c0):
        u = _dot(pbuf[...], wup_ref[:, c0:c0 + FF_CHUNK])
        cw = cw_ref[:, c0:c0 + FF_CHUNK]
        return (u[0:tm] * cw[0:1] + u[SUBLANES:tm + SUBLANES] * cw[1:2]
                + u[2 * SUBLANES:tm + 2 * SUBLANES] * cw[2:3] + cb_ref[:, c0:c0 + FF_CHUNK])

    for j in range(N_FF_CHUNKS):
        gate_j, value_j = conv_up(j * FF_CHUNK), conv_up(D_FF + j * FF_CHUNK)
        abuf[:, j * FF_CHUNK:(j + 1) * FF_CHUNK] = _bf(_silu(gate_j) * value_j)

    y = _dot(abuf[...], wdn_ref[...])
    for k in range(n_slabs):
        ybuf[k] = y[:, k * LANES:(k + 1) * LANES]
    gate = mod_ref[0][5:6]
    for s in range(SUBLANES):
        q, r0 = s // per_piece, (s % per_piece) * seg
        ys = jnp.concatenate(
            [jnp.concatenate([ybuf[k, pl.ds(s + SUBLANES * SUBLANES * m, SUBLANES, stride=SUBLANES), :]
                              for m in range(seg // SUBLANES)], axis=0) for k in range(n_slabs)], axis=1)
        x = x_ref[q, r0:r0 + seg, :] + gate * ys
        o_ref[q, r0:r0 + seg, :] = _rms(x, fg_ref[...]) if final else x


def _ffn_plan(x, h2, mod, per_batch_mod, lw, final_g, final, nq, rows):
    B, L, D = x.shape
    assert (nq == 1 and L % rows == 0) or (rows == L and B % nq == 0)
    tm = nq * rows
    seg = tm // SUBLANES
    tile = pl.BlockSpec((nq, rows, D), lambda b, i: (b, i, 0))
    halos = _halo_specs(rows, D, L)[1:] if nq == 1 else [pl.BlockSpec((1, HALO, D), lambda b, i: (b, 0, 0))] * 2
    mod_map = (lambda b, i: (b, 0, 0)) if per_batch_mod else (lambda b, i: (0, 0, 0))
    weights = [lw["w_up"], lw["ffn_cw"], lw["ffn_cb"], lw["w_down"], final_g]
    return dict(
        kernel=functools.partial(_ffn_kernel, nq, rows, final),
        grid=(B // nq, L // rows),
        in_specs=[tile, tile] + halos + [pl.BlockSpec((1, 8, D), mod_map)] + [_once(w) for w in weights],
        out_specs=[tile],
        out_shape=[jax.ShapeDtypeStruct((B, L, D), F32)],
        scratch_shapes=[pltpu.VMEM((nq * (rows + HALO) + HALO, D), BF16),
                        pltpu.VMEM((D // LANES, SUBLANES * (seg + 2 * HALO + 4), LANES), F32),
                        pltpu.VMEM(((seg + 2) * SUBLANES, D), BF16),
                        pltpu.VMEM((tm, D_FF), BF16),
                        pltpu.VMEM((D // LANES, tm, LANES), F32)],
        args=[x, h2, h2, h2, mod] + weights,
        name="conv_ffn")


def _layout_indices():
    sw = np.arange(ROPE_DIM) ^ (ROPE_DIM // 4)
    o3 = POOL_W + Q_LORA + KV_LORA
    o4 = o3 + ROPE_DIM
    idx_in = np.concatenate([np.arange(0, o3), np.arange(o4, IN_W), np.arange(o3, o4), o3 + sw])
    qh = NOPE_DIM + ROPE_DIM
    half = NOPE_DIM // 2
    idx_uq = np.concatenate([
        np.concatenate([h * qh + NOPE_DIM + np.arange(ROPE_DIM), h * qh + np.arange(half),
                        h * qh + NOPE_DIM + sw, h * qh + half + np.arange(half)])
        for h in range(N_HEADS)])
    kvh = NOPE_DIM + V_DIM
    zero_col = N_HEADS * kvh
    zeros = np.full(half, zero_col)
    idx_k = np.concatenate([
        np.concatenate([zeros, h * kvh + np.arange(half), zeros, h * kvh + half + np.arange(half)])
        for h in range(N_HEADS)])
    idx_v = np.concatenate([h * kvh + NOPE_DIM + np.arange(V_DIM) for h in range(N_HEADS)])
    return idx_in, idx_uq, np.concatenate([idx_k, idx_v])


def _take_cols(w, idx, zero_col):
    parts, start = [], 0
    for i in range(1, len(idx) + 1):
        run_ends = i == len(idx) or (idx[i] != idx[i - 1] + 1 and not (idx[i] == idx[i - 1] == zero_col))
        if run_ends:
            a, n = int(idx[start]), i - start
            parts.append(jnp.zeros((w.shape[0], n), BF16) if a == zero_col else _bf(w[:, a:a + n]))
            start = i
    return jnp.concatenate(parts, axis=1)


def _prep_layer(l, p):
    idx_in, idx_uq, idx_ukv = _layout_indices()
    row = lambda a: a[l][None, :]
    eye = jnp.eye(POOL_GROUPS, dtype=F32)
    pool_bd = (eye[:, None, :, None] * p["pool_w"][l][:, :, None, :]).reshape(POOL_W, POOL_W)
    return {
        "g1": row(p["norm1_g"]),
        "w_in": _take_cols(p["w_in"][l], idx_in, -1),
        "qg": row(p["q_norm_g"]),
        "w_uq": _take_cols(p["w_uq"][l], idx_uq, -1),
        "kvg": row(p["kv_norm_g"]),
        "w_ukv": _take_cols(p["w_ukv"][l], idx_ukv, N_HEADS * (NOPE_DIM + V_DIM)),
        "pool_bd": _bf(pool_bd),
        "pool_scale": row(p["pool_scale"]),
        "conv_w": jnp.pad(p["conv_w"][l], ((0, 1), (0, 0))),
        "conv_b": row(p["conv_b"]),
        "ln_g": row(p["conv_ln_g"]),
        "ln_b": row(p["conv_ln_b"]),
        "w_out": _bf(p["w_out"][l]),
        "g2": row(p["norm2_g"]),
        "w_up": _bf(p["w_up"][l]),
        "ffn_cw": jnp.pad(p["ffn_conv_w"][l], ((0, SUBLANES - FFN_K), (0, 0))),
        "ffn_cb": row(p["ffn_conv_b"]),
        "w_down": _bf(p["w_down"][l]),
    }


def _rope_table(L):
    rows = L // GRID_W
    row = np.repeat(np.arange(rows, dtype=np.float32), GRID_W)
    col = np.tile(np.arange(GRID_W, dtype=np.float32), rows)
    n = ROPE_DIM // 4
    inv = np.power(np.float32(ROPE_THETA), -np.arange(n, dtype=np.float32) / np.float32(n)).astype(np.float32)
    ang_r, ang_c = row[:, None] * inv, col[:, None] * inv
    cos = np.concatenate([np.cos(ang_r)] * 2 + [np.cos(ang_c)] * 2, axis=1)
    sin = np.concatenate([-np.sin(ang_r), np.sin(ang_r), -np.sin(ang_c), np.sin(ang_c)], axis=1)
    one, zero = np.ones_like(cos), np.zeros_like(cos)
    table = np.concatenate([cos, one, sin, zero, cos, zero], axis=1)
    return jnp.asarray(table, dtype=F32)


def kernel(x_prompt, x_sample, cache_ckv, cache_krope, c, c_ctx, ada_w, ada_b, norm1_g, w_in, pool_w,
           pool_scale, q_norm_g, w_uq, kv_norm_g, w_ukv, conv_w, conv_b, conv_ln_g, conv_ln_b, w_out,
           norm2_g, w_up, ffn_conv_w, ffn_conv_b, w_down, final_g):
    p = dict(norm1_g=norm1_g, w_in=w_in, pool_w=pool_w, pool_scale=pool_scale, q_norm_g=q_norm_g,
             w_uq=w_uq, kv_norm_g=kv_norm_g, w_ukv=w_ukv, conv_w=conv_w, conv_b=conv_b,
             conv_ln_g=conv_ln_g, conv_ln_b=conv_ln_b, w_out=w_out, norm2_g=norm2_g, w_up=w_up,
             ffn_conv_w=ffn_conv_w, ffn_conv_b=ffn_conv_b, w_down=w_down)
    n_dec = c.shape[0]
    c_all = jnp.concatenate([c, c_ctx[None, :], jnp.zeros((16 - n_dec - 1, D_MODEL), F32)], axis=0)
    mods = _mod_call(c_all, ada_w, ada_b.reshape(DEPTH, 1, 6 * D_MODEL))
    mods = jnp.pad(mods.reshape(DEPTH, 16, 6, D_MODEL), ((0, 0), (0, 0), (0, 2), (0, 0)))
    layers = [_prep_layer(l, p) for l in range(DEPTH)]
    fg = final_g[None, :]
    place = _bf(jnp.eye(ROPE_DIM, LANES, dtype=F32))
    rope_tab = _rope_table(x_sample.shape[1])

    def run(x, mod_of, per_batch_mod, rope, cache, tiles):
        states = []
        for l in range(DEPTH):
            lw, mod = layers[l], mod_of(l)
            outs = _inproj_call(x, mod, per_batch_mod, lw, rope, cache is None, tiles["inproj"])
            zp, zc, q, k, v = outs[:5]
            if cache is None:
                states.append(outs[5:])
                att = _attn_call(q, k, v, None, None, *tiles["attn"])
            else:
                kc, vc = _cachekv_call(cache[0], cache[1], l, lw["w_ukv"], place)
                att = _attn_call(q, k, v, kc, vc, *tiles["attn"])
            x, h2 = _launch(_mix_plan(x, zp, zc, att, mod, per_batch_mod, lw, tiles["mix"]))
            x, = _launch(_ffn_plan(x, h2, mod, per_batch_mod, lw, fg, l == DEPTH - 1, *tiles["ffn"]))
        return x, states

    y_prompt, states = run(x_prompt, lambda l: mods[l, n_dec:n_dec + 1], False, None, None, CTX_TILES)
    y_sample, _ = run(x_sample, lambda l: mods[l, :n_dec], True, rope_tab, (cache_ckv, cache_krope), DEC_TILES)
    state_ckv = jnp.stack([s[0] for s in states], axis=1)
    state_krope = jnp.stack([s[1] for s in states], axis=1)
    return (y_prompt, y_sample, state_ckv, state_krope)
```

```python
import functools
import math

import numpy as np
import jax
import jax.numpy as jnp
from jax import lax
from jax.experimental import pallas as pl
from jax.experimental.pallas import tpu as pltpu

D_MODEL = 1024
DEPTH = 4
GRID_W = 64
POOL_W = 256
POOL_GROUPS = 4
POOL_GW = 64
POOL_WINDOWS = (2, 4, 8, 16)
CONV_W = 256
CONV_K = 31
MLA_W = 512
N_HEADS = 4
V_DIM = 128
NOPE_DIM = 128
ROPE_DIM = 64
Q_LORA = 384
KV_LORA = 256
IN_W = POOL_W + Q_LORA + KV_LORA + ROPE_DIM + 2 * CONV_W
D_FF = 2816
FFN_K = 3
ROPE_THETA = 10000.0
EPS = 1e-6

LANES = 128
SUBLANES = 8
HEAD_W = 2 * LANES
IN_W_PAD = IN_W + ROPE_DIM
HALO = 16
FF_CHUNK = 256
KEY_CHUNK = 512
CACHE_BATCHES_PER_STEP = 4
N_FF_CHUNKS = D_FF // FF_CHUNK
VMEM_LIMIT = 56 * 1024 * 1024

CTX_TILES = {"inproj": 256, "attn": (256, 256, 4), "mix": 256, "ffn": (2, 256)}
DEC_TILES = {"inproj": 1024, "attn": (2048, 512, 1), "mix": 1024, "ffn": (1, 512)}

F32 = jnp.float32
BF16 = jnp.bfloat16


def _bf(x):
    return x.astype(BF16)


def _dot(a, b):
    return jnp.dot(a, b, preferred_element_type=F32)


def _rms(x, g):
    return x * lax.rsqrt(jnp.mean(x * x, axis=-1, keepdims=True) + EPS) * g


def _silu(x):
    return x * jax.nn.sigmoid(x)


def _params(n_parallel):
    return pltpu.CompilerParams(dimension_semantics=("parallel",) * n_parallel,
                                vmem_limit_bytes=VMEM_LIMIT)


def _mod_kernel(c_ref, w_ref, b_ref, o_ref):
    o_ref[0] = _dot(_bf(_silu(c_ref[...])), _bf(w_ref[0])) + b_ref[0]


def _mod_call(c_all, ada_w, ada_b):
    R = c_all.shape[0]
    tn = 1536
    return pl.pallas_call(
        _mod_kernel,
        grid=(DEPTH, 6 * D_MODEL // tn),
        in_specs=[pl.BlockSpec((R, D_MODEL), lambda l, j: (0, 0)),
                  pl.BlockSpec((1, D_MODEL, tn), lambda l, j: (l, 0, j)),
                  pl.BlockSpec((1, 1, tn), lambda l, j: (l, 0, j))],
        out_specs=pl.BlockSpec((1, R, tn), lambda l, j: (l, 0, j)),
        out_shape=jax.ShapeDtypeStruct((DEPTH, R, 6 * D_MODEL), F32),
        compiler_params=_params(2),
        name="adaln_mod",
    )(c_all, ada_w, ada_b)


def _inproj_kernel(rope, state, x_ref, mod_ref, g1_ref, win_ref, qg_ref, wuq_ref, kvg_ref, wukv_ref,
                   *rest):
    if rope:
        rope_ref, rest = rest[0], rest[1:]
    zp_ref, zc_ref, q_ref, k_ref, v_ref = rest[:5]
    x = x_ref[0]
    mod = mod_ref[0]
    h = _rms(x, g1_ref[...]) * (1.0 + mod[1:2]) + mod[0:1]
    z = _dot(_bf(h), win_ref[...])
    o1 = POOL_W
    o2 = o1 + Q_LORA
    o3 = o2 + KV_LORA
    o4 = o3 + 2 * CONV_W
    zp_ref[0] = z[:, :o1]
    zc_ref[0] = z[:, o3:o3 + CONV_W] * jax.nn.sigmoid(z[:, o3 + CONV_W:o4])
    kr = z[:, o4:]
    scale = math.log2(math.e) / math.sqrt(NOPE_DIM + ROPE_DIM)
    qn = _rms(z[:, o1:o2], qg_ref[...] * scale)
    q = _dot(_bf(qn), wuq_ref[...])
    ckv = _rms(z[:, o2:o3], kvg_ref[...])
    kv = _dot(_bf(ckv), wukv_ref[...])
    lane = lax.broadcasted_iota(jnp.int32, kr.shape, 1)
    if rope:
        tab = rope_ref[...]
        qa, qb, ka = tab[:, :LANES], tab[:, LANES:2 * LANES], tab[:, 2 * LANES:]
        krot = kr * ka + pltpu.roll(kr, ROPE_DIM, 1) * qb
    else:
        krot = jnp.where(lane < ROPE_DIM, kr, 0.0)
    for hd in range(N_HEADS):
        c0 = hd * HEAD_W
        q0 = q[:, c0:c0 + LANES]
        q1 = q[:, c0 + LANES:c0 + HEAD_W]
        if rope:
            q0 = q0 * qa + q1 * qb
        q_ref[0, :, c0:c0 + LANES] = _bf(q0)
        q_ref[0, :, c0 + LANES:c0 + HEAD_W] = _bf(q1)
        k_ref[0, :, c0:c0 + LANES] = _bf(kv[:, c0:c0 + LANES] + krot)
        k_ref[0, :, c0 + LANES:c0 + HEAD_W] = _bf(kv[:, c0 + LANES:c0 + HEAD_W])
    v_ref[0] = _bf(kv[:, N_HEADS * HEAD_W:].T)
    if state:
        ckv_ref, kr_ref = rest[5:7]
        ckv_ref[0] = ckv
        kr_ref[0] = kr[:, :ROPE_DIM]


def _inproj_call(x, mod, per_batch_mod, lw, rope_tab, state, tm):
    B, L, D = x.shape
    rope = rope_tab is not None
    tile = lambda w: pl.BlockSpec((1, tm, w), lambda b, i: (b, i, 0))
    mod_map = (lambda b, i: (b, 0, 0)) if per_batch_mod else (lambda b, i: (0, 0, 0))
    weights = [lw["g1"], lw["w_in"], lw["qg"], lw["w_uq"], lw["kvg"], lw["w_ukv"]]
    in_specs = [tile(D), pl.BlockSpec((1, 8, D), mod_map)] + [_once(w) for w in weights]
    args = [x, mod] + weights
    if rope:
        in_specs.append(pl.BlockSpec((tm, 3 * LANES), lambda b, i: (i, 0)))
        args.append(rope_tab)
    widths = [(POOL_W, F32), (CONV_W, F32), (N_HEADS * HEAD_W, BF16), (N_HEADS * HEAD_W, BF16)]
    states = [(KV_LORA, F32), (ROPE_DIM, F32)] if state else []
    vt_spec = pl.BlockSpec((1, N_HEADS * V_DIM, tm), lambda b, i: (b, 0, i))
    vt_shape = jax.ShapeDtypeStruct((B, N_HEADS * V_DIM, L), BF16)
    return pl.pallas_call(
        functools.partial(_inproj_kernel, rope, state),
        grid=(B, L // tm),
        in_specs=in_specs,
        out_specs=[tile(w) for w, _ in widths] + [vt_spec] + [tile(w) for w, _ in states],
        out_shape=[jax.ShapeDtypeStruct((B, L, w), dt) for w, dt in widths] + [vt_shape]
        + [jax.ShapeDtypeStruct((B, L, w), dt) for w, dt in states],
        compiler_params=_params(2),
        name="inproj",
    )(*args)


def _cachekv_kernel(ckv_ref, kr_ref, wukv_ref, place_ref, k_ref, v_ref):
    for b in range(ckv_ref.shape[0]):
        kv = _dot(_bf(ckv_ref[b, 0]), wukv_ref[...])
        krot = _dot(_bf(kr_ref[b, 0]), place_ref[...])
        for hd in range(N_HEADS):
            c0 = hd * HEAD_W
            k_ref[b, :, c0:c0 + LANES] = _bf(kv[:, c0:c0 + LANES] + krot)
            k_ref[b, :, c0 + LANES:c0 + HEAD_W] = _bf(kv[:, c0 + LANES:c0 + HEAD_W])
        v_ref[b] = _bf(kv[:, N_HEADS * HEAD_W:].T)


def _cachekv_call(cache_ckv, cache_krope, layer, w_ukv, place):
    B, _, Lc, _ = cache_ckv.shape
    nb = CACHE_BATCHES_PER_STEP
    return pl.pallas_call(
        _cachekv_kernel,
        grid=(B // nb,),
        in_specs=[pl.BlockSpec((nb, 1, Lc, KV_LORA), lambda b: (b, layer, 0, 0)),
                  pl.BlockSpec((nb, 1, Lc, ROPE_DIM), lambda b: (b, layer, 0, 0)),
                  pl.BlockSpec(w_ukv.shape, lambda b: (0, 0)),
                  pl.BlockSpec(place.shape, lambda b: (0, 0))],
        out_specs=[pl.BlockSpec((nb, Lc, N_HEADS * HEAD_W), lambda b: (b, 0, 0)),
                   pl.BlockSpec((nb, N_HEADS * V_DIM, Lc), lambda b: (b, 0, 0))],
        out_shape=[jax.ShapeDtypeStruct((B, Lc, N_HEADS * HEAD_W), BF16),
                   jax.ShapeDtypeStruct((B, N_HEADS * V_DIM, Lc), BF16)],
        compiler_params=_params(1),
        name="cache_kv",
    )(cache_ckv, cache_krope, w_ukv, place)


def _nt_dot(a, b):
    return lax.dot_general(a, b, (((1,), (1,)), ((), ())), preferred_element_type=F32)


def _attn_kernel(ctx, sub, heads, q_ref, k_ref, v_ref, *rest):
    if ctx:
        kc_ref, vc_ref, o_ref, s_buf = rest
    else:
        o_ref, s_buf = rest
    lk = k_ref.shape[1]
    ch = min(KEY_CHUNK, lk)
    chunks = [(k_ref, v_ref, r, r) for r in range(0, lk, ch)]
    if ctx:
        chunks += [(kc_ref, vc_ref, r, lk + r) for r in range(0, kc_ref.shape[1], ch)]
    items = [(hd, t) for hd in range(heads) for t in range(q_ref.shape[1] // sub)]

    def scores(i, c):
        hd, t = items[i]
        keys, _, r, row = chunks[c]
        s_buf[i % 2, row:row + ch, :] = _nt_dot(keys[0, r:r + ch, hd * HEAD_W:(hd + 1) * HEAD_W],
                                                q_ref[0, t * sub:(t + 1) * sub, hd * HEAD_W:(hd + 1) * HEAD_W])

    for c in range(len(chunks)):
        scores(0, c)
    for i, (hd, t) in enumerate(items):
        m = jnp.max(s_buf[i % 2], axis=0, keepdims=True)
        l = jnp.zeros((1, sub), F32)
        o = jnp.zeros((V_DIM, sub), F32)
        for c, (_, values, r, row) in enumerate(chunks):
            if i + 1 < len(items):
                scores(i + 1, c)
            e = jnp.exp2(s_buf[i % 2, row:row + ch, :] - m)
            l = l + jnp.sum(e, axis=0, keepdims=True)
            o = o + _dot(values[0, hd * V_DIM:(hd + 1) * V_DIM, r:r + ch], _bf(e))
        o_ref[0, t * sub:(t + 1) * sub, hd * V_DIM:(hd + 1) * V_DIM] = _bf((o / l).T)


def _attn_call(q, k, v, kc, vc, tq, sub, heads):
    B, L, _ = q.shape
    ctx = kc is not None
    k_spec = lambda a: pl.BlockSpec((1, a.shape[1], heads * HEAD_W), lambda b, h, i: (b, 0, h))
    v_spec = lambda a: pl.BlockSpec((1, heads * V_DIM, a.shape[2]), lambda b, h, i: (b, h, 0))
    in_specs = [pl.BlockSpec((1, tq, heads * HEAD_W), lambda b, h, i: (b, i, h)), k_spec(k), v_spec(v)]
    args = [q, k, v]
    if ctx:
        in_specs += [k_spec(kc), v_spec(vc)]
        args += [kc, vc]
    return pl.pallas_call(
        functools.partial(_attn_kernel, ctx, sub, heads),
        grid=(B, N_HEADS // heads, L // tq),
        in_specs=in_specs,
        out_specs=pl.BlockSpec((1, tq, heads * V_DIM), lambda b, h, i: (b, i, h)),
        out_shape=jax.ShapeDtypeStruct((B, L, N_HEADS * V_DIM), BF16),
        scratch_shapes=[pltpu.VMEM((2, k.shape[1] + (kc.shape[1] if ctx else 0), sub), F32)],
        compiler_params=_params(3),
        name="attention",
    )(*args)


def _halo_specs(tm, width, seq_len):
    per, last = tm // HALO, seq_len // HALO - 1
    return [pl.BlockSpec((1, tm, width), lambda b, i: (b, i, 0)),
            pl.BlockSpec((1, HALO, width), lambda b, i: (b, jnp.maximum(i * per - 1, 0), 0)),
            pl.BlockSpec((1, HALO, width), lambda b, i: (b, jnp.minimum((i + 1) * per, last), 0))]


def _halo_fill(dst_ref, tile_ref, left_ref, right_ref, i, n_tiles):
    tm = tile_ref.shape[1]
    zero = jnp.zeros((HALO, dst_ref.shape[1]), dst_ref.dtype)
    dst_ref[HALO:HALO + tm, :] = tile_ref[0]
    dst_ref[0:HALO, :] = jnp.where(i > 0, left_ref[0], zero)
    dst_ref[HALO + tm:, :] = jnp.where(i < n_tiles - 1, right_ref[0], zero)


def _mix_kernel(tm, seq_len, x_ref, zp_ref, zp_l, zp_r, zc_ref, zc_l, zc_r, att_ref, mod_ref, rc_ref, pw_ref,
                ps_ref, cw_ref, cb_ref, lng_ref, lnb_ref, wout_ref, g2_ref, xo_ref, h2_ref, pbuf, cbuf, cstage,
                wbuf, ybuf, ymix):
    i = pl.program_id(1)
    n_tiles = seq_len // tm
    _halo_fill(pbuf, zp_ref, zp_l, zp_r, i, n_tiles)
    _halo_fill(cbuf, zc_ref, zc_l, zc_r, i, n_tiles)

    z = pbuf[...]
    n = tm + 2 * HALO
    lane = lax.broadcasted_iota(jnp.int32, (tm, LANES), 1)
    zl, zr = z[:, :LANES], z[:, LANES:]
    a2l = zl[0:n - 1] + zl[1:n]
    a4l = a2l[0:n - 3] + a2l[2:n - 1]
    a2r = zr[0:n - 1] + zr[1:n]
    a4r = a2r[0:n - 3] + a2r[2:n - 1]
    a8r = a4r[0:n - 7] + a4r[4:n - 3]
    a16r = a8r[0:n - 15] + a8r[8:n - 7]
    low = lane < POOL_GW
    inv_count = rc_ref[...]
    pooled_l = jnp.where(low, a2l[HALO - 1:HALO - 1 + tm], a4l[HALO - 2:HALO - 2 + tm]) * inv_count[:, :LANES]
    pooled_r = jnp.where(low, a8r[HALO - 4:HALO - 4 + tm], a16r[HALO - 8:HALO - 8 + tm]) * inv_count[:, LANES:]
    zc_l, zc_r = zl[HALO:HALO + tm], zr[HALO:HALO + tm]
    d = jnp.concatenate([pooled_l - zc_l, pooled_r - zc_r], axis=1)
    o1, o2 = POOL_W, POOL_W + MLA_W
    ymix[:, 0:o1] = _bf(_dot(_bf(d), pw_ref[...]) * ps_ref[...])
    ymix[:, o1:o2] = att_ref[0]

    seg = tm // SUBLANES
    pitch = seg + 2 * HALO + 4
    n_slabs = CONV_W // LANES
    for s in range(SUBLANES):
        for k in range(n_slabs):
            cstage[k, s * pitch:s * pitch + seg + 2 * HALO, :] = (
                cbuf[s * seg:s * seg + seg + 2 * HALO, k * LANES:(k + 1) * LANES])
    for kk in range(CONV_K):
        wbuf[kk] = jnp.broadcast_to(cw_ref[kk:kk + 1, :], (SUBLANES, CONV_W))
    jb = 16
    first = HALO - CONV_K // 2
    cols = []
    for k in range(n_slabs):
        bias = jnp.broadcast_to(cb_ref[:, k * LANES:(k + 1) * LANES], (SUBLANES, LANES))
        groups = []
        for j0 in range(0, seg, jb):
            accs = [bias] * jb
            for p in range(j0, j0 + jb + CONV_K - 1):
                xrow = cstage[k, pl.ds(first + p, SUBLANES, stride=pitch), :]
                for j in range(max(j0, p - CONV_K + 1), min(j0 + jb - 1, p) + 1):
                    accs[j - j0] = accs[j - j0] + xrow * wbuf[p - j, :, k * LANES:(k + 1) * LANES]
            groups += accs
        cols.append(jnp.concatenate(groups, axis=0))
    u = jnp.concatenate(cols, axis=1)
    mu = jnp.mean(u, axis=-1, keepdims=True)
    var = jnp.mean(jnp.square(u - mu), axis=-1, keepdims=True)
    y_perm = _silu((u - mu) * lax.rsqrt(var + EPS) * lng_ref[...] + lnb_ref[...])
    for k in range(n_slabs):
        ybuf[k] = y_perm[:, k * LANES:(k + 1) * LANES]
    y_conv = jnp.concatenate(
        [jnp.concatenate([ybuf[k, pl.ds(s + SUBLANES * SUBLANES * m, SUBLANES, stride=SUBLANES), :]
                          for s in range(SUBLANES) for m in range(seg // SUBLANES)], axis=0)
         for k in range(n_slabs)], axis=1)
    ymix[:, o2:] = _bf(y_conv)

    mod = mod_ref[0]
    x = x_ref[0] + mod[2:3] * _dot(ymix[...], wout_ref[...])
    xo_ref[0] = x
    h2_ref[0] = _bf(_rms(x, g2_ref[...] * (1.0 + mod[4:5])) + mod[3:4])


def _once(a):
    return pl.BlockSpec(a.shape, lambda b, i: (0,) * a.ndim, pipeline_mode=pl.Buffered(1))


def _pool_inv_count(L):
    t = np.arange(L)
    cols = [np.repeat((1.0 / (np.minimum(t + w // 2, L) - np.maximum(t - w // 2, 0)))[:, None], POOL_GW, axis=1)
            for w in POOL_WINDOWS]
    return jnp.asarray(np.concatenate(cols, axis=1), dtype=F32)


def _mix_plan(x, zp, zc, att, mod, per_batch_mod, lw, tm):
    B, L, D = x.shape
    tile = lambda w: pl.BlockSpec((1, tm, w), lambda b, i: (b, i, 0))
    mod_map = (lambda b, i: (b, 0, 0)) if per_batch_mod else (lambda b, i: (0, 0, 0))
    weights = [lw["pool_bd"], lw["pool_scale"], lw["conv_w"], lw["conv_b"], lw["ln_g"], lw["ln_b"],
               lw["w_out"], lw["g2"]]
    return dict(
        kernel=functools.partial(_mix_kernel, tm, L),
        grid=(B, L // tm),
        in_specs=[tile(D)] + _halo_specs(tm, POOL_W, L) + _halo_specs(tm, CONV_W, L)
        + [tile(MLA_W), pl.BlockSpec((1, 8, D), mod_map), pl.BlockSpec((tm, POOL_W), lambda b, i: (i, 0))]
        + [_once(w) for w in weights],
        out_specs=[tile(D), tile(D)],
        out_shape=[jax.ShapeDtypeStruct((B, L, D), F32), jax.ShapeDtypeStruct((B, L, D), BF16)],
        scratch_shapes=[pltpu.VMEM((tm + 2 * HALO, POOL_W), F32), pltpu.VMEM((tm + 2 * HALO, CONV_W), F32),
                        pltpu.VMEM((CONV_W // LANES, SUBLANES * (tm // SUBLANES + 2 * HALO + 4), LANES), F32),
                        pltpu.VMEM((CONV_K + 1, SUBLANES, CONV_W), F32),
                        pltpu.VMEM((CONV_W // LANES, tm, LANES), F32),
                        pltpu.VMEM((tm, D), BF16)],
        args=[x, zp, zp, zp, zc, zc, zc, att, mod, _pool_inv_count(L)] + weights,
        name="mixer")


def _launch(plan):
    return pl.pallas_call(
        plan["kernel"], grid=plan["grid"], in_specs=plan["in_specs"], out_specs=plan["out_specs"],
        out_shape=plan["out_shape"], scratch_shapes=plan["scratch_shapes"],
        compiler_params=_params(len(plan["grid"])), name=plan["name"])(*plan["args"])


def _ffn_kernel(nq, rows, final, x_ref, h2_ref, h2_l, h2_r, mod_ref, wup_ref, cw_ref, cb_ref, wdn_ref, fg_ref,
                o_ref, hbuf, stage, pbuf, abuf, ybuf):
    tm = nq * rows
    seg = tm // SUBLANES
    per_piece = SUBLANES // nq
    pitch = seg + 2 * HALO + 4
    n_slabs = D_MODEL // LANES

    if nq == 1:
        _halo_fill(hbuf, h2_ref, h2_l, h2_r, pl.program_id(1), pl.num_programs(1))
    else:
        for q in range(nq + 1):
            hbuf[q * (rows + HALO):q * (rows + HALO) + HALO, :] = jnp.zeros((HALO, D_MODEL), BF16)
        for q in range(nq):
            hbuf[q * (rows + HALO) + HALO:(q + 1) * (rows + HALO), :] = h2_ref[q]

    for s in range(SUBLANES):
        src = (s // per_piece) * (rows + HALO) + (s % per_piece) * seg
        for k in range(n_slabs):
            stage[k, s * pitch:s * pitch + seg + 2 * HALO, :] = (
                hbuf[src:src + seg + 2 * HALO, k * LANES:(k + 1) * LANES].astype(F32))

    for g in range(seg // 2 + 1):
        for k in range(n_slabs):
            lo = stage[k, pl.ds(HALO - 1 + 2 * g, SUBLANES, stride=pitch), :]
            hi = stage[k, pl.ds(HALO + 2 * g, SUBLANES, stride=pitch), :]
            pbuf[g * 2 * SUBLANES:(g + 1) * 2 * SUBLANES, k * LANES:(k + 1) * LANES] = (
                _bf(jnp.concatenate([lo, hi], axis=0)))

    def conv_up(c0):
        u = _dot(pbuf[...], wup_ref[:, c0:c0 + FF_CHUNK])
        cw = cw_ref[:, c0:c0 + FF_CHUNK]
        return (u[0:tm] * cw[0:1] + u[SUBLANES:tm + SUBLANES] * cw[1:2]
                + u[2 * SUBLANES:tm + 2 * SUBLANES] * cw[2:3] + cb_ref[:, c0:c0 + FF_CHUNK])

    for j in range(N_FF_CHUNKS):
        gate_j, value_j = conv_up(j * FF_CHUNK), conv_up(D_FF + j * FF_CHUNK)
        abuf[:, j * FF_CHUNK:(j + 1) * FF_CHUNK] = _bf(_silu(gate_j) * value_j)

    y = _dot(abuf[...], wdn_ref[...])
    for k in range(n_slabs):
        ybuf[k] = y[:, k * LANES:(k + 1) * LANES]
    gate = mod_ref[0][5:6]
    for s in range(SUBLANES):
        q, r0 = s // per_piece, (s % per_piece) * seg
        ys = jnp.concatenate(
            [jnp.concatenate([ybuf[k, pl.ds(s + SUBLANES * SUBLANES * m, SUBLANES, stride=SUBLANES), :]
                              for m in range(seg // SUBLANES)], axis=0) for k in range(n_slabs)], axis=1)
        x = x_ref[q, r0:r0 + seg, :] + gate * ys
        o_ref[q, r0:r0 + seg, :] = _rms(x, fg_ref[...]) if final else x


def _ffn_plan(x, h2, mod, per_batch_mod, lw, final_g, final, nq, rows):
    B, L, D = x.shape
    assert (nq == 1 and L % rows == 0) or (rows == L and B % nq == 0)
    tm = nq * rows
    seg = tm // SUBLANES
    tile = pl.BlockSpec((nq, rows, D), lambda b, i: (b, i, 0))
    halos = _halo_specs(rows, D, L)[1:] if nq == 1 else [pl.BlockSpec((1, HALO, D), lambda b, i: (b, 0, 0))] * 2
    mod_map = (lambda b, i: (b, 0, 0)) if per_batch_mod else (lambda b, i: (0, 0, 0))
    weights = [lw["w_up"], lw["ffn_cw"], lw["ffn_cb"], lw["w_down"], final_g]
    return dict(
        kernel=functools.partial(_ffn_kernel, nq, rows, final),
        grid=(B // nq, L // rows),
        in_specs=[tile, tile] + halos + [pl.BlockSpec((1, 8, D), mod_map)] + [_once(w) for w in weights],
        out_specs=[tile],
        out_shape=[jax.ShapeDtypeStruct((B, L, D), F32)],
        scratch_shapes=[pltpu.VMEM((nq * (rows + HALO) + HALO, D), BF16),
                        pltpu.VMEM((D // LANES, SUBLANES * (seg + 2 * HALO + 4), LANES), F32),
                        pltpu.VMEM(((seg + 2) * SUBLANES, D), BF16),
                        pltpu.VMEM((tm, D_FF), BF16),
                        pltpu.VMEM((D // LANES, tm, LANES), F32)],
        args=[x, h2, h2, h2, mod] + weights,
        name="conv_ffn")


def _layout_indices():
    sw = np.arange(ROPE_DIM) ^ (ROPE_DIM // 4)
    o3 = POOL_W + Q_LORA + KV_LORA
    o4 = o3 + ROPE_DIM
    idx_in = np.concatenate([np.arange(0, o3), np.arange(o4, IN_W), np.arange(o3, o4), o3 + sw])
    qh = NOPE_DIM + ROPE_DIM
    half = NOPE_DIM // 2
    idx_uq = np.concatenate([
        np.concatenate([h * qh + NOPE_DIM + np.arange(ROPE_DIM), h * qh + np.arange(half),
                        h * qh + NOPE_DIM + sw, h * qh + half + np.arange(half)])
        for h in range(N_HEADS)])
    kvh = NOPE_DIM + V_DIM
    zero_col = N_HEADS * kvh
    zeros = np.full(half, zero_col)
    idx_k = np.concatenate([
        np.concatenate([zeros, h * kvh + np.arange(half), zeros, h * kvh + half + np.arange(half)])
        for h in range(N_HEADS)])
    idx_v = np.concatenate([h * kvh + NOPE_DIM + np.arange(V_DIM) for h in range(N_HEADS)])
    return idx_in, idx_uq, np.concatenate([idx_k, idx_v])


def _take_cols(w, idx, zero_col):
    parts, start = [], 0
    for i in range(1, len(idx) + 1):
        run_ends = i == len(idx) or (idx[i] != idx[i - 1] + 1 and not (idx[i] == idx[i - 1] == zero_col))
        if run_ends:
            a, n = int(idx[start]), i - start
            parts.append(jnp.zeros((w.shape[0], n), BF16) if a == zero_col else _bf(w[:, a:a + n]))
            start = i
    return jnp.concatenate(parts, axis=1)


def _prep_layer(l, p):
    idx_in, idx_uq, idx_ukv = _layout_indices()
    row = lambda a: a[l][None, :]
    eye = jnp.eye(POOL_GROUPS, dtype=F32)
    pool_bd = (eye[:, None, :, None] * p["pool_w"][l][:, :, None, :]).reshape(POOL_W, POOL_W)
    return {
        "g1": row(p["norm1_g"]),
        "w_in": _take_cols(p["w_in"][l], idx_in, -1),
        "qg": row(p["q_norm_g"]),
        "w_uq": _take_cols(p["w_uq"][l], idx_uq, -1),
        "kvg": row(p["kv_norm_g"]),
        "w_ukv": _take_cols(p["w_ukv"][l], idx_ukv, N_HEADS * (NOPE_DIM + V_DIM)),
        "pool_bd": _bf(pool_bd),
        "pool_scale": row(p["pool_scale"]),
        "conv_w": jnp.pad(p["conv_w"][l], ((0, 1), (0, 0))),
        "conv_b": row(p["conv_b"]),
        "ln_g": row(p["conv_ln_g"]),
        "ln_b": row(p["conv_ln_b"]),
        "w_out": _bf(p["w_out"][l]),
        "g2": row(p["norm2_g"]),
        "w_up": _bf(p["w_up"][l]),
        "ffn_cw": jnp.pad(p["ffn_conv_w"][l], ((0, SUBLANES - FFN_K), (0, 0))),
        "ffn_cb": row(p["ffn_conv_b"]),
        "w_down": _bf(p["w_down"][l]),
    }


def _rope_table(L):
    rows = L // GRID_W
    row = np.repeat(np.arange(rows, dtype=np.float32), GRID_W)
    col = np.tile(np.arange(GRID_W, dtype=np.float32), rows)
    n = ROPE_DIM // 4
    inv = np.power(np.float32(ROPE_THETA), -np.arange(n, dtype=np.float32) / np.float32(n)).astype(np.float32)
    ang_r, ang_c = row[:, None] * inv, col[:, None] * inv
    cos = np.concatenate([np.cos(ang_r)] * 2 + [np.cos(ang_c)] * 2, axis=1)
    sin = np.concatenate([-np.sin(ang_r), np.sin(ang_r), -np.sin(ang_c), np.sin(ang_c)], axis=1)
    one, zero = np.ones_like(cos), np.zeros_like(cos)
    table = np.concatenate([cos, one, sin, zero, cos, zero], axis=1)
    return jnp.asarray(table, dtype=F32)


def kernel(x_prompt, x_sample, cache_ckv, cache_krope, c, c_ctx, ada_w, ada_b, norm1_g, w_in, pool_w,
           pool_scale, q_norm_g, w_uq, kv_norm_g, w_ukv, conv_w, conv_b, conv_ln_g, conv_ln_b, w_out,
           norm2_g, w_up, ffn_conv_w, ffn_conv_b, w_down, final_g):
    p = dict(norm1_g=norm1_g, w_in=w_in, pool_w=pool_w, pool_scale=pool_scale, q_norm_g=q_norm_g,
             w_uq=w_uq, kv_norm_g=kv_norm_g, w_ukv=w_ukv, conv_w=conv_w, conv_b=conv_b,
             conv_ln_g=conv_ln_g, conv_ln_b=conv_ln_b, w_out=w_out, norm2_g=norm2_g, w_up=w_up,
             ffn_conv_w=ffn_conv_w, ffn_conv_b=ffn_conv_b, w_down=w_down)
    n_dec = c.shape[0]
    c_all = jnp.concatenate([c, c_ctx[None, :], jnp.zeros((16 - n_dec - 1, D_MODEL), F32)], axis=0)
    mods = _mod_call(c_all, ada_w, ada_b.reshape(DEPTH, 1, 6 * D_MODEL))
    mods = jnp.pad(mods.reshape(DEPTH, 16, 6, D_MODEL), ((0, 0), (0, 0), (0, 2), (0, 0)))
    layers = [_prep_layer(l, p) for l in range(DEPTH)]
    fg = final_g[None, :]
    place = _bf(jnp.eye(ROPE_DIM, LANES, dtype=F32))
    rope_tab = _rope_table(x_sample.shape[1])

    def run(x, mod_of, per_batch_mod, rope, cache, tiles):
        states = []
        for l in range(DEPTH):
            lw, mod = layers[l], mod_of(l)
            outs = _inproj_call(x, mod, per_batch_mod, lw, rope, cache is None, tiles["inproj"])
            zp, zc, q, k, v = outs[:5]
            if cache is None:
                states.append(outs[5:])
                att = _attn_call(q, k, v, None, None, *tiles["attn"])
            else:
                kc, vc = _cachekv_call(cache[0], cache[1], l, lw["w_ukv"], place)
                att = _attn_call(q, k, v, kc, vc, *tiles["attn"])
            x, h2 = _launch(_mix_plan(x, zp, zc, att, mod, per_batch_mod, lw, tiles["mix"]))
            x, = _launch(_ffn_plan(x, h2, mod, per_batch_mod, lw, fg, l == DEPTH - 1, *tiles["ffn"]))
        return x, states

    y_prompt, states = run(x_prompt, lambda l: mods[l, n_dec:n_dec + 1], False, None, None, CTX_TILES)
    y_sample, _ = run(x_sample, lambda l: mods[l, :n_dec], True, rope_tab, (cache_ckv, cache_krope), DEC_TILES)
    state_ckv = jnp.stack([s[0] for s in states], axis=1)
    state_krope = jnp.stack([s[1] for s in states], axis=1)
    return (y_prompt, y_sample, state_ckv, state_krope)
```

```python
import functools
import math

import numpy as np
import jax
import jax.numpy as jnp
from jax import lax
from jax.experimental import pallas as pl
from jax.experimental.pallas import tpu as pltpu

D_MODEL = 1024
DEPTH = 4
GRID_W = 64
POOL_W = 256
POOL_GROUPS = 4
POOL_GW = 64
POOL_WINDOWS = (2, 4, 8, 16)
CONV_W = 256
CONV_K = 31
MLA_W = 512
N_HEADS = 4
V_DIM = 128
NOPE_DIM = 128
ROPE_DIM = 64
Q_LORA = 384
KV_LORA = 256
IN_W = POOL_W + Q_LORA + KV_LORA + ROPE_DIM + 2 * CONV_W
D_FF = 2816
FFN_K = 3
ROPE_THETA = 10000.0
EPS = 1e-6

LANES = 128
SUBLANES = 8
HEAD_W = 2 * LANES
IN_W_PAD = IN_W + ROPE_DIM
HALO = 16
FF_CHUNK = 256
KEY_CHUNK = 512
CACHE_BATCHES_PER_STEP = 4
N_FF_CHUNKS = D_FF // FF_CHUNK
VMEM_LIMIT = 56 * 1024 * 1024

CTX_TILES = {"inproj": 256, "attn": (256, 256, 4), "mix": 256, "ffn": (2, 256)}
DEC_TILES = {"inproj": 1024, "attn": (2048, 512, 1), "mix": 1024, "ffn": (1, 512)}

F32 = jnp.float32
BF16 = jnp.bfloat16


def _bf(x):
    return x.astype(BF16)


def _dot(a, b):
    return jnp.dot(a, b, preferred_element_type=F32)


def _rms(x, g):
    return x * lax.rsqrt(jnp.mean(x * x, axis=-1, keepdims=True) + EPS) * g


def _silu(x):
    return x * jax.nn.sigmoid(x)


def _params(n_parallel):
    return pltpu.CompilerParams(dimension_semantics=("parallel",) * n_parallel,
                                vmem_limit_bytes=VMEM_LIMIT)


def _mod_kernel(c_ref, w_ref, b_ref, o_ref):
    o_ref[0] = _dot(_bf(_silu(c_ref[...])), _bf(w_ref[0])) + b_ref[0]


def _mod_call(c_all, ada_w, ada_b):
    R = c_all.shape[0]
    tn = 1536
    return pl.pallas_call(
        _mod_kernel,
        grid=(DEPTH, 6 * D_MODEL // tn),
        in_specs=[pl.BlockSpec((R, D_MODEL), lambda l, j: (0, 0)),
                  pl.BlockSpec((1, D_MODEL, tn), lambda l, j: (l, 0, j)),
                  pl.BlockSpec((1, 1, tn), lambda l, j: (l, 0, j))],
        out_specs=pl.BlockSpec((1, R, tn), lambda l, j: (l, 0, j)),
        out_shape=jax.ShapeDtypeStruct((DEPTH, R, 6 * D_MODEL), F32),
        compiler_params=_params(2),
        name="adaln_mod",
    )(c_all, ada_w, ada_b)


def _inproj_kernel(rope, state, x_ref, mod_ref, g1_ref, win_ref, qg_ref, wuq_ref, kvg_ref, wukv_ref,
                   *rest):
    if rope:
        rope_ref, rest = rest[0], rest[1:]
    zp_ref, zc_ref, q_ref, k_ref, v_ref = rest[:5]
    x = x_ref[0]
    mod = mod_ref[0]
    h = _rms(x, g1_ref[...] * (1.0 + mod[1:2])) + mod[0:1]
    z = _dot(_bf(h), win_ref[...])
    o1 = POOL_W
    o2 = o1 + Q_LORA
    o3 = o2 + KV_LORA
    o4 = o3 + 2 * CONV_W
    zp_ref[0] = z[:, :o1]
    zc_ref[0] = z[:, o3:o3 + CONV_W] * jax.nn.sigmoid(z[:, o3 + CONV_W:o4])
    kr = z[:, o4:]
    scale = math.log2(math.e) / math.sqrt(NOPE_DIM + ROPE_DIM)
    qn = _rms(z[:, o1:o2], qg_ref[...] * scale)
    q = _dot(_bf(qn), wuq_ref[...])
    ckv = _rms(z[:, o2:o3], kvg_ref[...])
    kv = _dot(_bf(ckv), wukv_ref[...])
    lane = lax.broadcasted_iota(jnp.int32, kr.shape, 1)
    if rope:
        tab = rope_ref[...]
        qa, qb, ka = tab[:, :LANES], tab[:, LANES:2 * LANES], tab[:, 2 * LANES:]
        krot = kr * ka + pltpu.roll(kr, ROPE_DIM, 1) * qb
    else:
        krot = jnp.where(lane < ROPE_DIM, kr, 0.0)
    for hd in range(N_HEADS):
        c0 = hd * HEAD_W
        q0 = q[:, c0:c0 + LANES]
        q1 = q[:, c0 + LANES:c0 + HEAD_W]
        if rope:
            q0 = q0 * qa + q1 * qb
        q_ref[0, :, c0:c0 + LANES] = _bf(q0)
        q_ref[0, :, c0 + LANES:c0 + HEAD_W] = _bf(q1)
        k_ref[0, :, c0:c0 + LANES] = _bf(kv[:, c0:c0 + LANES] + krot)
        k_ref[0, :, c0 + LANES:c0 + HEAD_W] = _bf(kv[:, c0 + LANES:c0 + HEAD_W])
    v_ref[0] = _bf(kv[:, N_HEADS * HEAD_W:].T)
    if state:
        ckv_ref, kr_ref = rest[5:7]
        ckv_ref[0] = ckv
        kr_ref[0] = kr[:, :ROPE_DIM]


def _inproj_call(x, mod, per_batch_mod, lw, rope_tab, state, tm):
    B, L, D = x.shape
    rope = rope_tab is not None
    tile = lambda w: pl.BlockSpec((1, tm, w), lambda b, i: (b, i, 0))
    mod_map = (lambda b, i: (b, 0, 0)) if per_batch_mod else (lambda b, i: (0, 0, 0))
    weights = [lw["g1"], lw["w_in"], lw["qg"], lw["w_uq"], lw["kvg"], lw["w_ukv"]]
    in_specs = [tile(D), pl.BlockSpec((1, 8, D), mod_map)] + [_once(w) for w in weights]
    args = [x, mod] + weights
    if rope:
        in_specs.append(pl.BlockSpec((tm, 3 * LANES), lambda b, i: (i, 0)))
        args.append(rope_tab)
    widths = [(POOL_W, F32), (CONV_W, F32), (N_HEADS * HEAD_W, BF16), (N_HEADS * HEAD_W, BF16)]
    states = [(KV_LORA, F32), (ROPE_DIM, F32)] if state else []
    vt_spec = pl.BlockSpec((1, N_HEADS * V_DIM, tm), lambda b, i: (b, 0, i))
    vt_shape = jax.ShapeDtypeStruct((B, N_HEADS * V_DIM, L), BF16)
    return pl.pallas_call(
        functools.partial(_inproj_kernel, rope, state),
        grid=(B, L // tm),
        in_specs=in_specs,
        out_specs=[tile(w) for w, _ in widths] + [vt_spec] + [tile(w) for w, _ in states],
        out_shape=[jax.ShapeDtypeStruct((B, L, w), dt) for w, dt in widths] + [vt_shape]
        + [jax.ShapeDtypeStruct((B, L, w), dt) for w, dt in states],
        compiler_params=_params(2),
        name="inproj",
    )(*args)


def _cachekv_kernel(ckv_ref, kr_ref, wukv_ref, place_ref, k_ref, v_ref):
    for b in range(ckv_ref.shape[0]):
        kv = _dot(_bf(ckv_ref[b, 0]), wukv_ref[...])
        krot = _dot(_bf(kr_ref[b, 0]), place_ref[...])
        for hd in range(N_HEADS):
            c0 = hd * HEAD_W
            k_ref[b, :, c0:c0 + LANES] = _bf(kv[:, c0:c0 + LANES] + krot)
            k_ref[b, :, c0 + LANES:c0 + HEAD_W] = _bf(kv[:, c0 + LANES:c0 + HEAD_W])
        v_ref[b] = _bf(kv[:, N_HEADS * HEAD_W:].T)


def _cachekv_call(cache_ckv, cache_krope, layer, w_ukv, place):
    B, _, Lc, _ = cache_ckv.shape
    nb = CACHE_BATCHES_PER_STEP
    return pl.pallas_call(
        _cachekv_kernel,
        grid=(B // nb,),
        in_specs=[pl.BlockSpec((nb, 1, Lc, KV_LORA), lambda b: (b, layer, 0, 0)),
                  pl.BlockSpec((nb, 1, Lc, ROPE_DIM), lambda b: (b, layer, 0, 0)),
                  pl.BlockSpec(w_ukv.shape, lambda b: (0, 0)),
                  pl.BlockSpec(place.shape, lambda b: (0, 0))],
        out_specs=[pl.BlockSpec((nb, Lc, N_HEADS * HEAD_W), lambda b: (b, 0, 0)),
                   pl.BlockSpec((nb, N_HEADS * V_DIM, Lc), lambda b: (b, 0, 0))],
        out_shape=[jax.ShapeDtypeStruct((B, Lc, N_HEADS * HEAD_W), BF16),
                   jax.ShapeDtypeStruct((B, N_HEADS * V_DIM, Lc), BF16)],
        compiler_params=_params(1),
        name="cache_kv",
    )(cache_ckv, cache_krope, w_ukv, place)


def _nt_dot(a, b):
    return lax.dot_general(a, b, (((1,), (1,)), ((), ())), preferred_element_type=F32)


def _attn_kernel(ctx, sub, heads, q_ref, k_ref, v_ref, *rest):
    if ctx:
        kc_ref, vc_ref, o_ref, s_buf = rest
    else:
        o_ref, s_buf = rest
    lk = k_ref.shape[1]
    ch = min(KEY_CHUNK, lk)
    chunks = [(k_ref, v_ref, r, r) for r in range(0, lk, ch)]
    if ctx:
        chunks += [(kc_ref, vc_ref, r, lk + r) for r in range(0, kc_ref.shape[1], ch)]
    items = [(hd, t) for hd in range(heads) for t in range(q_ref.shape[1] // sub)]

    def scores(i, c):
        hd, t = items[i]
        keys, _, r, row = chunks[c]
        s_buf[i % 2, row:row + ch, :] = _nt_dot(keys[0, r:r + ch, hd * HEAD_W:(hd + 1) * HEAD_W],
                                                q_ref[0, t * sub:(t + 1) * sub, hd * HEAD_W:(hd + 1) * HEAD_W])

    for c in range(len(chunks)):
        scores(0, c)
    for i, (hd, t) in enumerate(items):
        m = jnp.max(s_buf[i % 2], axis=0, keepdims=True)
        l = jnp.zeros((1, sub), F32)
        o = jnp.zeros((V_DIM, sub), F32)
        for c, (_, values, r, row) in enumerate(chunks):
            if i + 1 < len(items):
                scores(i + 1, c)
            e = jnp.exp2(s_buf[i % 2, row:row + ch, :] - m)
            l = l + jnp.sum(e, axis=0, keepdims=True)
            o = o + _dot(values[0, hd * V_DIM:(hd + 1) * V_DIM, r:r + ch], _bf(e))
        o_ref[0, t * sub:(t + 1) * sub, hd * V_DIM:(hd + 1) * V_DIM] = _bf((o / l).T)


def _attn_call(q, k, v, kc, vc, tq, sub, heads):
    B, L, _ = q.shape
    ctx = kc is not None
    k_spec = lambda a: pl.BlockSpec((1, a.shape[1], heads * HEAD_W), lambda b, h, i: (b, 0, h))
    v_spec = lambda a: pl.BlockSpec((1, heads * V_DIM, a.shape[2]), lambda b, h, i: (b, h, 0))
    in_specs = [pl.BlockSpec((1, tq, heads * HEAD_W), lambda b, h, i: (b, i, h)), k_spec(k), v_spec(v)]
    args = [q, k, v]
    if ctx:
        in_specs += [k_spec(kc), v_spec(vc)]
        args += [kc, vc]
    return pl.pallas_call(
        functools.partial(_attn_kernel, ctx, sub, heads),
        grid=(B, N_HEADS // heads, L // tq),
        in_specs=in_specs,
        out_specs=pl.BlockSpec((1, tq, heads * V_DIM), lambda b, h, i: (b, i, h)),
        out_shape=jax.ShapeDtypeStruct((B, L, N_HEADS * V_DIM), BF16),
        scratch_shapes=[pltpu.VMEM((2, k.shape[1] + (kc.shape[1] if ctx else 0), sub), F32)],
        compiler_params=_params(3),
        name="attention",
    )(*args)


def _halo_specs(tm, width, seq_len):
    per, last = tm // HALO, seq_len // HALO - 1
    return [pl.BlockSpec((1, tm, width), lambda b, i: (b, i, 0)),
            pl.BlockSpec((1, HALO, width), lambda b, i: (b, jnp.maximum(i * per - 1, 0), 0)),
            pl.BlockSpec((1, HALO, width), lambda b, i: (b, jnp.minimum((i + 1) * per, last), 0))]


def _halo_fill(dst_ref, tile_ref, left_ref, right_ref, i, n_tiles):
    tm = tile_ref.shape[1]
    zero = jnp.zeros((HALO, dst_ref.shape[1]), dst_ref.dtype)
    dst_ref[HALO:HALO + tm, :] = tile_ref[0]
    dst_ref[0:HALO, :] = jnp.where(i > 0, left_ref[0], zero)
    dst_ref[HALO + tm:, :] = jnp.where(i < n_tiles - 1, right_ref[0], zero)


def _mix_kernel(tm, seq_len, x_ref, zp_ref, zp_l, zp_r, zc_ref, zc_l, zc_r, att_ref, mod_ref, rc_ref, pw_ref,
                ps_ref, cw_ref, cb_ref, lng_ref, lnb_ref, wout_ref, g2_ref, xo_ref, h2_ref, pbuf, cbuf, cstage,
                wbuf, ybuf, ymix):
    i = pl.program_id(1)
    n_tiles = seq_len // tm
    _halo_fill(pbuf, zp_ref, zp_l, zp_r, i, n_tiles)
    _halo_fill(cbuf, zc_ref, zc_l, zc_r, i, n_tiles)

    z = pbuf[...]
    n = tm + 2 * HALO
    lane = lax.broadcasted_iota(jnp.int32, (tm, LANES), 1)
    zl, zr = z[:, :LANES], z[:, LANES:]
    a2l = zl[0:n - 1] + zl[1:n]
    a4l = a2l[0:n - 3] + a2l[2:n - 1]
    a2r = zr[0:n - 1] + zr[1:n]
    a4r = a2r[0:n - 3] + a2r[2:n - 1]
    a8r = a4r[0:n - 7] + a4r[4:n - 3]
    a16r = a8r[0:n - 15] + a8r[8:n - 7]
    low = lane < POOL_GW
    inv_count = rc_ref[...]
    pooled_l = jnp.where(low, a2l[HALO - 1:HALO - 1 + tm], a4l[HALO - 2:HALO - 2 + tm]) * inv_count[:, :LANES]
    pooled_r = jnp.where(low, a8r[HALO - 4:HALO - 4 + tm], a16r[HALO - 8:HALO - 8 + tm]) * inv_count[:, LANES:]
    zc_l, zc_r = zl[HALO:HALO + tm], zr[HALO:HALO + tm]
    d = jnp.concatenate([pooled_l - zc_l, pooled_r - zc_r], axis=1)
    o1, o2 = POOL_W, POOL_W + MLA_W
    ymix[:, 0:o1] = _bf(_dot(_bf(d), pw_ref[...]) * ps_ref[...])
    ymix[:, o1:o2] = att_ref[0]

    seg = tm // SUBLANES
    pitch = seg + 2 * HALO + 4
    n_slabs = CONV_W // LANES
    for s in range(SUBLANES):
        for k in range(n_slabs):
            cstage[k, s * pitch:s * pitch + seg + 2 * HALO, :] = (
                cbuf[s * seg:s * seg + seg + 2 * HALO, k * LANES:(k + 1) * LANES])
    for kk in range(CONV_K):
        wbuf[kk] = jnp.broadcast_to(cw_ref[kk:kk + 1, :], (SUBLANES, CONV_W))
    jb = 16
    first = HALO - CONV_K // 2
    cols = []
    for k in range(n_slabs):
        bias = jnp.broadcast_to(cb_ref[:, k * LANES:(k + 1) * LANES], (SUBLANES, LANES))
        groups = []
        for j0 in range(0, seg, jb):
            accs = [bias] * jb
            for p in range(j0, j0 + jb + CONV_K - 1):
                xrow = cstage[k, pl.ds(first + p, SUBLANES, stride=pitch), :]
                for j in range(max(j0, p - CONV_K + 1), min(j0 + jb - 1, p) + 1):
                    accs[j - j0] = accs[j - j0] + xrow * wbuf[p - j, :, k * LANES:(k + 1) * LANES]
            groups += accs
        cols.append(jnp.concatenate(groups, axis=0))
    u = jnp.concatenate(cols, axis=1)
    mu = jnp.mean(u, axis=-1, keepdims=True)
    var = jnp.mean(jnp.square(u - mu), axis=-1, keepdims=True)
    y_perm = _silu((u - mu) * lax.rsqrt(var + EPS) * lng_ref[...] + lnb_ref[...])
    for k in range(n_slabs):
        ybuf[k] = y_perm[:, k * LANES:(k + 1) * LANES]
    y_conv = jnp.concatenate(
        [jnp.concatenate([ybuf[k, pl.ds(s + SUBLANES * SUBLANES * m, SUBLANES, stride=SUBLANES), :]
                          for s in range(SUBLANES) for m in range(seg // SUBLANES)], axis=0)
         for k in range(n_slabs)], axis=1)
    ymix[:, o2:] = _bf(y_conv)

    mod = mod_ref[0]
    x = x_ref[0] + mod[2:3] * _dot(ymix[...], wout_ref[...])
    xo_ref[0] = x
    h2_ref[0] = _bf(_rms(x, g2_ref[...] * (1.0 + mod[4:5])) + mod[3:4])


def _once(a):
    return pl.BlockSpec(a.shape, lambda b, i: (0,) * a.ndim, pipeline_mode=pl.Buffered(1))


def _pool_inv_count(L):
    t = np.arange(L)
    cols = [np.repeat((1.0 / (np.minimum(t + w // 2, L) - np.maximum(t - w // 2, 0)))[:, None], POOL_GW, axis=1)
            for w in POOL_WINDOWS]
    return jnp.asarray(np.concatenate(cols, axis=1), dtype=F32)


def _mix_plan(x, zp, zc, att, mod, per_batch_mod, lw, tm):
    B, L, D = x.shape
    tile = lambda w: pl.BlockSpec((1, tm, w), lambda b, i: (b, i, 0))
    mod_map = (lambda b, i: (b, 0, 0)) if per_batch_mod else (lambda b, i: (0, 0, 0))
    weights = [lw["pool_bd"], lw["pool_scale"], lw["conv_w"], lw["conv_b"], lw["ln_g"], lw["ln_b"],
               lw["w_out"], lw["g2"]]
    return dict(
        kernel=functools.partial(_mix_kernel, tm, L),
        grid=(B, L // tm),
        in_specs=[tile(D)] + _halo_specs(tm, POOL_W, L) + _halo_specs(tm, CONV_W, L)
        + [tile(MLA_W), pl.BlockSpec((1, 8, D), mod_map), pl.BlockSpec((tm, POOL_W), lambda b, i: (i, 0))]
        + [_once(w) for w in weights],
        out_specs=[tile(D), tile(D)],
        out_shape=[jax.ShapeDtypeStruct((B, L, D), F32), jax.ShapeDtypeStruct((B, L, D), BF16)],
        scratch_shapes=[pltpu.VMEM((tm + 2 * HALO, POOL_W), F32), pltpu.VMEM((tm + 2 * HALO, CONV_W), F32),
                        pltpu.VMEM((CONV_W // LANES, SUBLANES * (tm // SUBLANES + 2 * HALO + 4), LANES), F32),
                        pltpu.VMEM((CONV_K + 1, SUBLANES, CONV_W), F32),
                        pltpu.VMEM((CONV_W // LANES, tm, LANES), F32),
                        pltpu.VMEM((tm, D), BF16)],
        args=[x, zp, zp, zp, zc, zc, zc, att, mod, _pool_inv_count(L)] + weights,
        name="mixer")


def _launch(plan):
    return pl.pallas_call(
        plan["kernel"], grid=plan["grid"], in_specs=plan["in_specs"], out_specs=plan["out_specs"],
        out_shape=plan["out_shape"], scratch_shapes=plan["scratch_shapes"],
        compiler_params=_params(len(plan["grid"])), name=plan["name"])(*plan["args"])


def _ffn_kernel(nq, rows, final, x_ref, h2_ref, h2_l, h2_r, mod_ref, wup_ref, cw_ref, cb_ref, wdn_ref, fg_ref,
                o_ref, hbuf, stage, pbuf, abuf, ybuf):
    tm = nq * rows
    seg = tm // SUBLANES
    per_piece = SUBLANES // nq
    pitch = seg + 2 * HALO + 4
    n_slabs = D_MODEL // LANES

    if nq == 1:
        _halo_fill(hbuf, h2_ref, h2_l, h2_r, pl.program_id(1), pl.num_programs(1))
    else:
        for q in range(nq + 1):
            hbuf[q * (rows + HALO):q * (rows + HALO) + HALO, :] = jnp.zeros((HALO, D_MODEL), BF16)
        for q in range(nq):
            hbuf[q * (rows + HALO) + HALO:(q + 1) * (rows + HALO), :] = h2_ref[q]

    for s in range(SUBLANES):
        src = (s // per_piece) * (rows + HALO) + (s % per_piece) * seg
        for k in range(n_slabs):
            stage[k, s * pitch:s * pitch + seg + 2 * HALO, :] = (
                hbuf[src:src + seg + 2 * HALO, k * LANES:(k + 1) * LANES].astype(F32))

    for g in range(seg // 2 + 1):
        for k in range(n_slabs):
            lo = stage[k, pl.ds(HALO - 1 + 2 * g, SUBLANES, stride=pitch), :]
            hi = stage[k, pl.ds(HALO + 2 * g, SUBLANES, stride=pitch), :]
            pbuf[g * 2 * SUBLANES:(g + 1) * 2 * SUBLANES, k * LANES:(k + 1) * LANES] = (
                _bf(jnp.concatenate([lo, hi], axis=0)))

    def conv_up(c0):
        u = _dot(pbuf[...], wup_ref[:, c0:c0 + FF_CHUNK])
        cw = cw_ref[:, c0:c0 + FF_CHUNK]
        return (u[0:tm] * cw[0:1] + u[SUBLANES:tm + SUBLANES] * cw[1:2]
                + u[2 * SUBLANES:tm + 2 * SUBLANES] * cw[2:3] + cb_ref[:, c0:c0 + FF_CHUNK])

    for j in range(N_FF_CHUNKS):
        gate_j, value_j = conv_up(j * FF_CHUNK), conv_up(D_FF + j * FF_CHUNK)
        abuf[:, j * FF_CHUNK:(j + 1) * FF_CHUNK] = _bf(_silu(gate_j) * value_j)

    y = _dot(abuf[...], wdn_ref[...])
    for k in range(n_slabs):
        ybuf[k] = y[:, k * LANES:(k + 1) * LANES]
    gate = mod_ref[0][5:6]
    for s in range(SUBLANES):
        q, r0 = s // per_piece, (s % per_piece) * seg
        ys = jnp.concatenate(
            [jnp.concatenate([ybuf[k, pl.ds(s + SUBLANES * SUBLANES * m, SUBLANES, stride=SUBLANES), :]
                              for m in range(seg // SUBLANES)], axis=0) for k in range(n_slabs)], axis=1)
        x = x_ref[q, r0:r0 + seg, :] + gate * ys
        o_ref[q, r0:r0 + seg, :] = _rms(x, fg_ref[...]) if final else x


def _ffn_plan(x, h2, mod, per_batch_mod, lw, final_g, final, nq, rows):
    B, L, D = x.shape
    assert (nq == 1 and L % rows == 0) or (rows == L and B % nq == 0)
    tm = nq * rows
    seg = tm // SUBLANES
    tile = pl.BlockSpec((nq, rows, D), lambda b, i: (b, i, 0))
    halos = _halo_specs(rows, D, L)[1:] if nq == 1 else [pl.BlockSpec((1, HALO, D), lambda b, i: (b, 0, 0))] * 2
    mod_map = (lambda b, i: (b, 0, 0)) if per_batch_mod else (lambda b, i: (0, 0, 0))
    weights = [lw["w_up"], lw["ffn_cw"], lw["ffn_cb"], lw["w_down"], final_g]
    return dict(
        kernel=functools.partial(_ffn_kernel, nq, rows, final),
        grid=(B // nq, L // rows),
        in_specs=[tile, tile] + halos + [pl.BlockSpec((1, 8, D), mod_map)] + [_once(w) for w in weights],
        out_specs=[tile],
        out_shape=[jax.ShapeDtypeStruct((B, L, D), F32)],
        scratch_shapes=[pltpu.VMEM((nq * (rows + HALO) + HALO, D), BF16),
                        pltpu.VMEM((D // LANES, SUBLANES * (seg + 2 * HALO + 4), LANES), F32),
                        pltpu.VMEM(((seg + 2) * SUBLANES, D), BF16),
                        pltpu.VMEM((tm, D_FF), BF16),
                        pltpu.VMEM((D // LANES, tm, LANES), F32)],
        args=[x, h2, h2, h2, mod] + weights,
        name="conv_ffn")


def _layout_indices():
    sw = np.arange(ROPE_DIM) ^ (ROPE_DIM // 4)
    o3 = POOL_W + Q_LORA + KV_LORA
    o4 = o3 + ROPE_DIM
    idx_in = np.concatenate([np.arange(0, o3), np.arange(o4, IN_W), np.arange(o3, o4), o3 + sw])
    qh = NOPE_DIM + ROPE_DIM
    half = NOPE_DIM // 2
    idx_uq = np.concatenate([
        np.concatenate([h * qh + NOPE_DIM + np.arange(ROPE_DIM), h * qh + np.arange(half),
                        h * qh + NOPE_DIM + sw, h * qh + half + np.arange(half)])
        for h in range(N_HEADS)])
    kvh = NOPE_DIM + V_DIM
    zero_col = N_HEADS * kvh
    zeros = np.full(half, zero_col)
    idx_k = np.concatenate([
        np.concatenate([zeros, h * kvh + np.arange(half), zeros, h * kvh + half + np.arange(half)])
        for h in range(N_HEADS)])
    idx_v = np.concatenate([h * kvh + NOPE_DIM + np.arange(V_DIM) for h in range(N_HEADS)])
    return idx_in, idx_uq, np.concatenate([idx_k, idx_v])


def _take_cols(w, idx, zero_col):
    parts, start = [], 0
    for i in range(1, len(idx) + 1):
        run_ends = i == len(idx) or (idx[i] != idx[i - 1] + 1 and not (idx[i] == idx[i - 1] == zero_col))
        if run_ends:
            a, n = int(idx[start]), i - start
            parts.append(jnp.zeros((w.shape[0], n), BF16) if a == zero_col else _bf(w[:, a:a + n]))
            start = i
    return jnp.concatenate(parts, axis=1)


def _prep_layer(l, p):
    idx_in, idx_uq, idx_ukv = _layout_indices()
    row = lambda a: a[l][None, :]
    eye = jnp.eye(POOL_GROUPS, dtype=F32)
    pool_bd = (eye[:, None, :, None] * p["pool_w"][l][:, :, None, :]).reshape(POOL_W, POOL_W)
    return {
        "g1": row(p["norm1_g"]),
        "w_in": _take_cols(p["w_in"][l], idx_in, -1),
        "qg": row(p["q_norm_g"]),
        "w_uq": _take_cols(p["w_uq"][l], idx_uq, -1),
        "kvg": row(p["kv_norm_g"]),
        "w_ukv": _take_cols(p["w_ukv"][l], idx_ukv, N_HEADS * (NOPE_DIM + V_DIM)),
        "pool_bd": _bf(pool_bd),
        "pool_scale": row(p["pool_scale"]),
        "conv_w": jnp.pad(p["conv_w"][l], ((0, 1), (0, 0))),
        "conv_b": row(p["conv_b"]),
        "ln_g": row(p["conv_ln_g"]),
        "ln_b": row(p["conv_ln_b"]),
        "w_out": _bf(p["w_out"][l]),
        "g2": row(p["norm2_g"]),
        "w_up": _bf(p["w_up"][l]),
        "ffn_cw": jnp.pad(p["ffn_conv_w"][l], ((0, SUBLANES - FFN_K), (0, 0))),
        "ffn_cb": row(p["ffn_conv_b"]),
        "w_down": _bf(p["w_down"][l]),
    }


def _rope_table(L):
    rows = L // GRID_W
    row = np.repeat(np.arange(rows, dtype=np.float32), GRID_W)
    col = np.tile(np.arange(GRID_W, dtype=np.float32), rows)
    n = ROPE_DIM // 4
    inv = np.power(np.float32(ROPE_THETA), -np.arange(n, dtype=np.float32) / np.float32(n)).astype(np.float32)
    ang_r, ang_c = row[:, None] * inv, col[:, None] * inv
    cos = np.concatenate([np.cos(ang_r)] * 2 + [np.cos(ang_c)] * 2, axis=1)
    sin = np.concatenate([-np.sin(ang_r), np.sin(ang_r), -np.sin(ang_c), np.sin(ang_c)], axis=1)
    one, zero = np.ones_like(cos), np.zeros_like(cos)
    table = np.concatenate([cos, one, sin, zero, cos, zero], axis=1)
    return jnp.asarray(table, dtype=F32)


def kernel(x_prompt, x_sample, cache_ckv, cache_krope, c, c_ctx, ada_w, ada_b, norm1_g, w_in, pool_w,
           pool_scale, q_norm_g, w_uq, kv_norm_g, w_ukv, conv_w, conv_b, conv_ln_g, conv_ln_b, w_out,
           norm2_g, w_up, ffn_conv_w, ffn_conv_b, w_down, final_g):
    p = dict(norm1_g=norm1_g, w_in=w_in, pool_w=pool_w, pool_scale=pool_scale, q_norm_g=q_norm_g,
             w_uq=w_uq, kv_norm_g=kv_norm_g, w_ukv=w_ukv, conv_w=conv_w, conv_b=conv_b,
             conv_ln_g=conv_ln_g, conv_ln_b=conv_ln_b, w_out=w_out, norm2_g=norm2_g, w_up=w_up,
             ffn_conv_w=ffn_conv_w, ffn_conv_b=ffn_conv_b, w_down=w_down)
    n_dec = c.shape[0]
    c_all = jnp.concatenate([c, c_ctx[None, :], jnp.zeros((16 - n_dec - 1, D_MODEL), F32)], axis=0)
    mods = _mod_call(c_all, ada_w, ada_b.reshape(DEPTH, 1, 6 * D_MODEL))
    mods = jnp.pad(mods.reshape(DEPTH, 16, 6, D_MODEL), ((0, 0), (0, 0), (0, 2), (0, 0)))
    layers = [_prep_layer(l, p) for l in range(DEPTH)]
    fg = final_g[None, :]
    place = _bf(jnp.eye(ROPE_DIM, LANES, dtype=F32))
    rope_tab = _rope_table(x_sample.shape[1])

    def run(x, mod_of, per_batch_mod, rope, cache, tiles):
        states = []
        for l in range(DEPTH):
            lw, mod = layers[l], mod_of(l)
            outs = _inproj_call(x, mod, per_batch_mod, lw, rope, cache is None, tiles["inproj"])
            zp, zc, q, k, v = outs[:5]
            if cache is None:
                states.append(outs[5:])
                att = _attn_call(q, k, v, None, None, *tiles["attn"])
            else:
                kc, vc = _cachekv_call(cache[0], cache[1], l, lw["w_ukv"], place)
                att = _attn_call(q, k, v, kc, vc, *tiles["attn"])
            x, h2 = _launch(_mix_plan(x, zp, zc, att, mod, per_batch_mod, lw, tiles["mix"]))
            x, = _launch(_ffn_plan(x, h2, mod, per_batch_mod, lw, fg, l == DEPTH - 1, *tiles["ffn"]))
        return x, states

    y_prompt, states = run(x_prompt, lambda l: mods[l, n_dec:n_dec + 1], False, None, None, CTX_TILES)
    y_sample, _ = run(x_sample, lambda l: mods[l, :n_dec], True, rope_tab, (cache_ckv, cache_krope), DEC_TILES)
    state_ckv = jnp.stack([s[0] for s in states], axis=1)
    state_krope = jnp.stack([s[1] for s in states], axis=1)
    return (y_prompt, y_sample, state_ckv, state_krope)
```
